```python
import math
import jax, jax.numpy as jnp
from jax import lax
import numpy as np

D_MODEL = 1024
BATCH = 32
SEQ = 2048
DEPTH = 2

N_MIXERS = 2
N_META = 16
Q_BLOCK = 128
SB_HEADS = 16
SB_HEAD_DIM = D_MODEL // SB_HEADS
DIFF_HEAD_DIM = 64
DIFF_HEADS = D_MODEL // (2 * DIFF_HEAD_DIM)
ROPE_THETA = 10000.0
D_FF = 2816
N_EXPERTS = 8
TOP_K = 2
D_FF_EXPERT = 3584
NORM_EPS = 1e-6
N_SB = (DEPTH + 1) // 2
N_DIFF = DEPTH // 2
N_DENSE = (DEPTH + 1) // 2
N_MOE = DEPTH // 2

kernel_name = "hybrid_stickbreaking_diffattn_moe"


def _rmsnorm(x, g):
    xf = x.astype(jnp.float32)
    y = xf * lax.rsqrt(jnp.mean(xf * xf, axis=-1, keepdims=True) + NORM_EPS)
    return (y * g.astype(jnp.float32)).astype(x.dtype)


def _sweep_query_blocks(block_fn, q):
    b, l = q.shape[0], q.shape[1]
    rest = q.shape[2:]
    n_blk = (l - N_META) // Q_BLOCK
    o_meta = block_fn(q[:, :N_META], jnp.arange(N_META, dtype=jnp.int32))
    q_real = jnp.moveaxis(q[:, N_META:].reshape((b, n_blk, Q_BLOCK) + rest), 1, 0)
    pos = N_META + jnp.arange(n_blk * Q_BLOCK, dtype=jnp.int32).reshape(n_blk, Q_BLOCK)
    o_real = lax.map(lambda a: block_fn(a[0], a[1]), (q_real, pos))
    o_real = jnp.moveaxis(o_real, 0, 1)
    o_real = o_real.reshape((b, n_blk * Q_BLOCK) + o_real.shape[3:])
    return jnp.concatenate([o_meta, o_real], axis=1)


def _stick_breaking_mixer(h, w_qkv, w_o):
    b, l, _ = h.shape
    q, k, v = jnp.split(h @ w_qkv, 3, axis=-1)
    q = q.reshape(b, l, SB_HEADS, SB_HEAD_DIM)
    k = k.reshape(b, l, SB_HEADS, SB_HEAD_DIM)
    v = v.reshape(b, l, SB_HEADS, SB_HEAD_DIM)
    scale = SB_HEAD_DIM ** -0.5
    k_pos = jnp.arange(l, dtype=jnp.int32)

    def block(qb, q_pos):
        z = jnp.einsum('bqhd,bkhd->bhqk', qb, k).astype(jnp.float32) * scale
        strict = k_pos[None, :] < q_pos[:, None]
        log_beta = jax.nn.log_sigmoid(z)
        log_keep = jnp.where(strict, jax.nn.log_sigmoid(-z), 0.0)
        log_keep_after = lax.cumsum(log_keep, axis=3, reverse=True) - log_keep
        a = jnp.where(strict, jnp.exp(log_beta + log_keep_after), 0.0)
        return jnp.einsum('bhqk,bkhd->bqhd', a.astype(v.dtype), v)

    o = _sweep_query_blocks(block, q)
    return o.reshape(b, l, D_MODEL) @ w_o


def _rope_tables(l):
    inv = ROPE_THETA ** (-jnp.arange(0, DIFF_HEAD_DIM, 2, dtype=jnp.float32) / DIFF_HEAD_DIM)
    ang = jnp.arange(l, dtype=jnp.float32)[:, None] * inv[None, :]
    return jnp.cos(ang), jnp.sin(ang)


def _apply_rope(x, cos, sin):
    c = cos[:, None, None, :]
    s = sin[:, None, None, :]
    xf = x.astype(jnp.float32)
    x1, x2 = jnp.split(xf, 2, axis=-1)
    return jnp.concatenate([x1 * c - x2 * s, x2 * c + x1 * s], axis=-1).astype(x.dtype)


def _diff_mixer(h, w_qkv, w_o, subln_g, lq1, lk1, lq2, lk2, lambda_init):
    b, l, _ = h.shape
    q, k, v = jnp.split(h @ w_qkv, 3, axis=-1)
    q = q.reshape(b, l, DIFF_HEADS, 2, DIFF_HEAD_DIM)
    k = k.reshape(b, l, DIFF_HEADS, 2, DIFF_HEAD_DIM)
    v = v.reshape(b, l, DIFF_HEADS, 2 * DIFF_HEAD_DIM)
    cos, sin = _rope_tables(l)
    q = _apply_rope(q, cos, sin)
    k = _apply_rope(k, cos, sin)
    lam = (jnp.exp(jnp.sum(lq1.astype(jnp.float32) * lk1.astype(jnp.float32)))
           - jnp.exp(jnp.sum(lq2.astype(jnp.float32) * lk2.astype(jnp.float32)))
           + lambda_init)
    scale = DIFF_HEAD_DIM ** -0.5
    k_pos = jnp.arange(l, dtype=jnp.int32)

    def block(qb, q_pos):
        s = jnp.einsum('bqhcd,bkhcd->bhcqk', qb, k).astype(jnp.float32) * scale
        causal = k_pos[None, :] <= q_pos[:, None]
        p = jax.nn.softmax(jnp.where(causal, s, -jnp.inf), axis=-1)
        w = p[:, :, 0] - lam * p[:, :, 1]
        return jnp.einsum('bhqk,bkhe->bqhe', w.astype(v.dtype), v)

    o = _sweep_query_blocks(block, q)
    o = _rmsnorm(o, subln_g) * (1.0 - lambda_init)
    return o.reshape(b, l, D_MODEL) @ w_o


def _swiglu(x, w_gate, w_up, w_down):
    return (jax.nn.silu(x @ w_gate) * (x @ w_up)) @ w_down


def _moe(h, w_router, w_gate, w_up, w_down):
    b, l, d = h.shape
    x = h.reshape(b * l, d)
    logits = (x @ w_router).astype(jnp.float32)
    top_v, top_i = lax.top_k(logits, TOP_K)
    gates = jax.nn.softmax(top_v, axis=-1)
    combine = jnp.sum(jax.nn.one_hot(top_i, N_EXPERTS, dtype=jnp.float32) * gates[..., None], axis=1)
    out = jnp.zeros_like(x)
    for e in range(N_EXPERTS):
        out = out + combine[:, e:e + 1].astype(x.dtype) * _swiglu(x, w_gate[e], w_up[e], w_down[e])
    return out.reshape(b, l, d)


def setup_inputs(seed: int = 0) -> dict:
    key = jax.random.key(seed)
    ks = jax.random.split(key, 20)
    f32 = jnp.float32

    def w(k, shape, fan_in):
        return jax.random.normal(k, shape, f32) * (fan_in ** -0.5)

    return {
        "x": jax.random.normal(ks[0], (BATCH, SEQ, D_MODEL), f32),
        "meta_tokens": jax.random.normal(ks[1], (N_META, D_MODEL), f32),
        "norm_gains": 1.0 + 0.05 * jax.random.normal(ks[2], (DEPTH, 2, D_MODEL), f32),
        "sb_w_qkv": w(ks[3], (N_SB, D_MODEL, 3 * D_MODEL), D_MODEL),
        "sb_w_o": w(ks[4], (N_SB, D_MODEL, D_MODEL), D_MODEL),
        "diff_w_qkv": w(ks[5], (N_DIFF, D_MODEL, 3 * D_MODEL), D_MODEL),
        "diff_w_o": w(ks[6], (N_DIFF, D_MODEL, D_MODEL), D_MODEL),
        "diff_subln": 1.0 + 0.05 * jax.random.normal(ks[7], (N_DIFF, 2 * DIFF_HEAD_DIM), f32),
        "diff_lambda_q1": 0.1 * jax.random.normal(ks[8], (N_DIFF, DIFF_HEAD_DIM), f32),
        "diff_lambda_k1": 0.1 * jax.random.normal(ks[9], (N_DIFF, DIFF_HEAD_DIM), f32),
        "diff_lambda_q2": 0.1 * jax.random.normal(ks[10], (N_DIFF, DIFF_HEAD_DIM), f32),
        "diff_lambda_k2": 0.1 * jax.random.normal(ks[11], (N_DIFF, DIFF_HEAD_DIM), f32),
        "ffn_w_gate": w(ks[12], (N_DENSE, D_MODEL, D_FF), D_MODEL),
        "ffn_w_up": w(ks[13], (N_DENSE, D_MODEL, D_FF), D_MODEL),
        "ffn_w_down": w(ks[14], (N_DENSE, D_FF, D_MODEL), D_FF),
        "moe_router": w(ks[15], (N_MOE, D_MODEL, N_EXPERTS), D_MODEL),
        "moe_w_gate": w(ks[16], (N_MOE, N_EXPERTS, D_MODEL, D_FF_EXPERT), D_MODEL),
        "moe_w_up": w(ks[17], (N_MOE, N_EXPERTS, D_MODEL, D_FF_EXPERT), D_MODEL),
        "moe_w_down": w(ks[18], (N_MOE, N_EXPERTS, D_FF_EXPERT, D_MODEL), D_FF_EXPERT),
        "final_norm": 1.0 + 0.05 * jax.random.normal(ks[19], (D_MODEL,), f32),
    }


def reference(x, meta_tokens, norm_gains, sb_w_qkv, sb_w_o, diff_w_qkv, diff_w_o, diff_subln,
              diff_lambda_q1, diff_lambda_k1, diff_lambda_q2, diff_lambda_k2,
              ffn_w_gate, ffn_w_up, ffn_w_down, moe_router, moe_w_gate, moe_w_up, moe_w_down,
              final_norm):
    b = x.shape[0]
    meta = jnp.broadcast_to(meta_tokens.astype(x.dtype)[None], (b, N_META, x.shape[-1]))
    h = jnp.concatenate([meta, x], axis=1)
    for i in range(DEPTH):
        j = i // N_MIXERS
        hn = _rmsnorm(h, norm_gains[i, 0])
        if i % N_MIXERS == 0:
            h = h + _stick_breaking_mixer(hn, sb_w_qkv[j], sb_w_o[j])
        else:
            lambda_init = 0.8 - 0.6 * math.exp(-0.3 * i)
            h = h + _diff_mixer(hn, diff_w_qkv[j], diff_w_o[j], diff_subln[j],
                                diff_lambda_q1[j], diff_lambda_k1[j],
                                diff_lambda_q2[j], diff_lambda_k2[j], lambda_init)
        hn = _rmsnorm(h, norm_gains[i, 1])
        if i % 2 == 0:
            h = h + _swiglu(hn, ffn_w_gate[j], ffn_w_up[j], ffn_w_down[j])
        else:
            h = h + _moe(hn, moe_router[j], moe_w_gate[j], moe_w_up[j], moe_w_down[j])
    h = _rmsnorm(h, final_norm)
    return h[:, N_META:]
```

```python
import functools
import math

import jax
import jax.numpy as jnp
from jax import lax
from jax.experimental import pallas as pl
from jax.experimental.pallas import tpu as pltpu

F32 = jnp.float32
BF16 = jnp.bfloat16

NORM_EPS = 1e-6
ROPE_THETA = 10000.0
HEAD_DIM = 64
LANES = 128
QB = 128
ROW_TILE = 512
EXPERT_TILE = 1024
EXPERT_FF_CHUNK = 512
FF_CHUNK = 256
NEG_BIG = -1e30
VMEM_LIMIT = 56 * 1024 * 1024


def _cparams(sem):
    return pltpu.CompilerParams(dimension_semantics=sem, vmem_limit_bytes=VMEM_LIMIT)


def _rmsnorm(x, g):
    ms = jnp.mean(x * x, axis=-1, keepdims=True)
    return x * lax.rsqrt(ms + NORM_EPS) * g


def _dot(a, b):
    return jnp.dot(a, b, preferred_element_type=F32)


def _dot_t(a, b):
    return lax.dot_general(a, b, (((1,), (1,)), ((), ())), preferred_element_type=F32)


def _split_bf16(x):
    hi = x.astype(BF16)
    lo = (x - hi.astype(F32)).astype(BF16)
    return hi, lo


def _resident(shape):
    nd = len(shape)
    return pl.BlockSpec(shape, lambda *_: (0,) * nd, pipeline_mode=pl.Buffered(1))


def _norm_matmul_kernel(h_ref, g_ref, w_ref, o_ref, *, n_chunk):
    hn = _rmsnorm(h_ref[...], g_ref[...]).astype(BF16)
    for c in range(0, w_ref.shape[1], n_chunk):
        o_ref[:, c:c + n_chunk] = _dot(hn, w_ref[:, c:c + n_chunk]).astype(o_ref.dtype)


def _norm_matmul(h, g, w):
    t, d = h.shape
    n = w.shape[1]
    return pl.pallas_call(
        functools.partial(_norm_matmul_kernel, n_chunk=min(n, 1024)),
        out_shape=jax.ShapeDtypeStruct((t, n), BF16),
        grid=(t // ROW_TILE,),
        in_specs=[pl.BlockSpec((ROW_TILE, d), lambda i: (i, 0)),
                  _resident((1, d)),
                  _resident((d, n))],
        out_specs=pl.BlockSpec((ROW_TILE, n), lambda i: (i, 0)),
        compiler_params=_cparams(("parallel",)),
        name="norm_qkv",
    )(h, g.reshape(1, d), w)


def _sb_attn_kernel(q_ref, k_ref, v_ref, o_ref, *, seq, scale):
    nq = seq // QB
    tail = seq - nq * QB
    lane = lax.broadcasted_iota(jnp.int32, (1, LANES), 1)
    head_lanes = (lane < HEAD_DIM, lane >= HEAD_DIM)
    rr = lax.broadcasted_iota(jnp.int32, (QB, QB), 0)
    cc = lax.broadcasted_iota(jnp.int32, (QB, QB), 1)
    after = (rr > cc).astype(BF16)

    def chunk(qh, k0, c_run, acc, mask):
        kb = k_ref[pl.ds(k0, QB), :]
        vb = v_ref[pl.ds(k0, QB), :]
        z = _dot_t(qh, kb) * scale
        soft = jnp.log(1.0 + jnp.exp(-jnp.abs(z)))
        log_beta = jnp.minimum(z, 0.0) - soft
        log_keep = log_beta - z
        if mask is not None:
            log_keep = jnp.where(mask, log_keep, 0.0)
        hi, lo = _split_bf16(log_keep)
        keep_after = _dot(hi, after) + _dot(lo, after)
        p = jnp.exp(log_beta + keep_after + c_run)
        if mask is not None:
            p = jnp.where(mask, p, 0.0)
        acc = acc + _dot(p.astype(BF16), vb)
        c_run = c_run + jnp.sum(log_keep, axis=1, keepdims=True)
        return c_run, acc

    def q_block(q0, rows, diag_k0, diag_mask, n_below):
        qb = q_ref[pl.ds(q0, rows), :]
        qhs = [jnp.where(m, qb, jnp.zeros_like(qb)) for m in head_lanes]
        state = []
        for qh in qhs:
            c0 = jnp.zeros((rows, 1), F32)
            a0 = jnp.zeros((rows, LANES), F32)
            state.extend(chunk(qh, diag_k0, c0, a0, diag_mask))

        def below(idx, st):
            k0 = pl.multiple_of((n_below - 1 - idx) * QB, QB)
            out = []
            for h, qh in enumerate(qhs):
                out.extend(chunk(qh, k0, st[2 * h], st[2 * h + 1], None))
            return tuple(out)

        state = lax.fori_loop(0, n_below, below, tuple(state))
        o = jnp.where(head_lanes[0], state[1], state[3])
        o_ref[pl.ds(q0, rows), :] = o.astype(o_ref.dtype)

    def q_loop(i, carry):
        q0 = pl.multiple_of(i * QB, QB)
        q_block(q0, QB, q0, cc < rr, i)
        return carry

    lax.fori_loop(0, nq, q_loop, 0)

    if tail:
        q0 = nq * QB
        k0 = seq - QB
        t_idx = q0 + lax.broadcasted_iota(jnp.int32, (tail, QB), 0)
        s_idx = k0 + lax.broadcasted_iota(jnp.int32, (tail, QB), 1)
        q_block(q0, tail, k0, (s_idx >= q0) & (s_idx < t_idx), nq)


def _sb_attention(qkv, batch, seq, d_model):
    qkv3 = qkv.reshape(batch, seq, 3 * d_model)
    nblk = d_model // LANES
    spec = lambda off: pl.BlockSpec((None, seq, LANES), lambda b, h: (b, 0, off + h))
    out = pl.pallas_call(
        functools.partial(_sb_attn_kernel, seq=seq, scale=HEAD_DIM ** -0.5),
        out_shape=jax.ShapeDtypeStruct((batch, seq, d_model), BF16),
        grid=(batch, nblk),
        in_specs=[spec(0), spec(nblk), spec(2 * nblk)],
        out_specs=spec(0),
        compiler_params=_cparams(("parallel", "parallel")),
        name="sb_attention",
    )(qkv3, qkv3, qkv3)
    return out.reshape(batch * seq, d_model)


def _diff_attn_kernel(q_ref, k_ref, v_ref, cos_ref, sin_ref, lam_ref, subg_ref, o_ref, krot_ref,
                      *, seq, scale, lambda_init):
    nq = seq // QB
    tail = seq - nq * QB
    lane = lax.broadcasted_iota(jnp.int32, (1, LANES), 1)
    map_lanes = (lane < HEAD_DIM, lane >= HEAD_DIM)
    first_half = (lane & (HEAD_DIM // 2)) == 0
    rr = lax.broadcasted_iota(jnp.int32, (QB, QB), 0)
    cc = lax.broadcasted_iota(jnp.int32, (QB, QB), 1)

    lam_p = lam_ref[...]
    lam = (jnp.exp(jnp.sum(lam_p[0:1] * lam_p[1:2], axis=-1, keepdims=True))
           - jnp.exp(jnp.sum(lam_p[2:3] * lam_p[3:4], axis=-1, keepdims=True))
           + lambda_init)

    def rope(x, r0, rows):
        partner = jnp.where(first_half, pltpu.roll(x, LANES - HEAD_DIM // 2, 1),
                            pltpu.roll(x, HEAD_DIM // 2, 1))
        return x * cos_ref[pl.ds(r0, rows), :] + partner * sin_ref[pl.ds(r0, rows), :]

    def rot_keys(r0, rows):
        krot_ref[pl.ds(r0, rows), :] = rope(k_ref[pl.ds(r0, rows), :].astype(F32), r0, rows).astype(BF16)

    def rot_loop(i, carry):
        rot_keys(pl.multiple_of(i * QB, QB), QB)
        return carry

    lax.fori_loop(0, nq, rot_loop, 0)
    if tail:
        rot_keys(nq * QB, tail)

    def chunk(qc, k0, m, l, acc, mask):
        kb = krot_ref[pl.ds(k0, QB), :]
        vb = v_ref[pl.ds(k0, QB), :]
        s = _dot_t(qc, kb)
        if mask is not None:
            s = jnp.where(mask, s, NEG_BIG)
        m_new = jnp.maximum(m, jnp.max(s, axis=1, keepdims=True))
        alpha = jnp.exp(m - m_new)
        p = jnp.exp(s - m_new)
        l = alpha * l + jnp.sum(p, axis=1, keepdims=True)
        acc = alpha * acc + _dot(p.astype(BF16), vb)
        return m_new, l, acc

    def q_block(q0, rows, diag_k0, diag_mask, n_below):
        qb = (rope(q_ref[pl.ds(q0, rows), :].astype(F32), q0, rows) * scale).astype(BF16)
        qcs = [jnp.where(m, qb, jnp.zeros_like(qb)) for m in map_lanes]
        state = []
        for qc in qcs:
            m0 = jnp.full((rows, 1), NEG_BIG, F32)
            l0 = jnp.zeros((rows, 1), F32)
            a0 = jnp.zeros((rows, LANES), F32)
            state.extend(chunk(qc, diag_k0, m0, l0, a0, diag_mask))

        def below(idx, st):
            k0 = pl.multiple_of(idx * QB, QB)
            out = []
            for c, qc in enumerate(qcs):
                out.extend(chunk(qc, k0, st[3 * c], st[3 * c + 1], st[3 * c + 2], None))
            return tuple(out)

        st = lax.fori_loop(0, n_below, below, tuple(state))
        o = st[2] / st[1] - lam * (st[5] / st[4])
        o = _rmsnorm(o, subg_ref[...]) * (1.0 - lambda_init)
        o_ref[pl.ds(q0, rows), :] = o.astype(o_ref.dtype)

    def q_loop(i, carry):
        q0 = pl.multiple_of(i * QB, QB)
        q_block(q0, QB, q0, cc <= rr, i)
        return carry

    lax.fori_loop(0, nq, q_loop, 0)

    if tail:
        q0 = nq * QB
        k0 = seq - QB
        t_idx = q0 + lax.broadcasted_iota(jnp.int32, (tail, QB), 0)
        s_idx = k0 + lax.broadcasted_iota(jnp.int32, (tail, QB), 1)
        q_block(q0, tail, k0, (s_idx >= q0) & (s_idx <= t_idx), nq)


def _rope_tables(seq):
    half = HEAD_DIM // 2
    inv = ROPE_THETA ** (-jnp.arange(0, HEAD_DIM, 2, dtype=F32) / HEAD_DIM)
    ang = jnp.arange(seq, dtype=F32)[:, None] * inv[None, :]
    cos, sin = jnp.cos(ang), jnp.sin(ang)
    reps = LANES // HEAD_DIM
    cos_t = jnp.tile(cos, (1, 2 * reps))
    sin_t = jnp.tile(jnp.concatenate([-sin, sin], axis=1), (1, reps))
    assert cos_t.shape == (seq, LANES) and sin_t.shape == (seq, LANES) and half * 2 == HEAD_DIM
    return cos_t, sin_t


def _diff_attention(qkv, lam_params, subln_g, batch, seq, d_model, lambda_init):
    qkv3 = qkv.reshape(batch, seq, 3 * d_model)
    nblk = d_model // LANES
    cos_t, sin_t = _rope_tables(seq)
    spec = lambda off: pl.BlockSpec((None, seq, LANES), lambda b, h: (b, 0, off + h))
    out = pl.pallas_call(
        functools.partial(_diff_attn_kernel, seq=seq, scale=HEAD_DIM ** -0.5, lambda_init=lambda_init),
        out_shape=jax.ShapeDtypeStruct((batch, seq, d_model), BF16),
        grid=(batch, nblk),
        in_specs=[spec(0), spec(nblk), spec(2 * nblk),
                  _resident((seq, LANES)), _resident((seq, LANES)),
                  _resident(lam_params.shape), _resident((1, LANES))],
        out_specs=spec(0),
        scratch_shapes=[pltpu.VMEM((seq, LANES), BF16)],
        compiler_params=_cparams(("parallel", "parallel")),
        name="diff_attention",
    )(qkv3, qkv3, qkv3, cos_t, sin_t, lam_params, subln_g.reshape(1, LANES))
    return out.reshape(batch * seq, d_model)


def _silu_mul(g, u):
    return g * (1.0 / (1.0 + jnp.exp(-g))) * u


def _wo_ffn_kernel(o_ref, h_ref, wo_ref, g_ref, wg_ref, wu_ref, wd_ref, out_ref, a_ref, *, f_chunk):
    h1 = h_ref[...] + _dot(o_ref[...], wo_ref[...])
    hn = _rmsnorm(h1, g_ref[...]).astype(BF16)
    for c in range(0, wg_ref.shape[1], f_chunk):
        gate = _dot(hn, wg_ref[:, c:c + f_chunk])
        up = _dot(hn, wu_ref[:, c:c + f_chunk])
        a_ref[:, c:c + f_chunk] = _silu_mul(gate, up).astype(BF16)
    out_ref[...] = h1 + _dot(a_ref[...], wd_ref[...])


def _wo_ffn(o, h, wo, g, wg, wu, wd):
    t, d = h.shape
    f = wg.shape[1]
    f_chunk = FF_CHUNK if f % FF_CHUNK == 0 else f
    row = lambda: pl.BlockSpec((ROW_TILE, d), lambda i: (i, 0))
    return pl.pallas_call(
        functools.partial(_wo_ffn_kernel, f_chunk=f_chunk),
        out_shape=jax.ShapeDtypeStruct((t, d), F32),
        grid=(t // ROW_TILE,),
        in_specs=[row(), row(), _resident((d, d)), _resident((1, d)),
                  _resident((d, f)), _resident((d, f)), _resident((f, d))],
        out_specs=row(),
        scratch_shapes=[pltpu.VMEM((ROW_TILE, f), BF16)],
        compiler_params=_cparams(("parallel",)),
        name="wo_ffn",
    )(o, h, wo, g.reshape(1, d), wg, wu, wd)


def _wo_router_kernel(o_ref, h_ref, wo_ref, g_ref, wr_hi_ref, wr_lo_ref,
                      h1_ref, hn_ref, gate_ref, meta_ref, cnt_ref, run_ref, *, n_exp):
    rows = h_ref.shape[0]

    @pl.when(pl.program_id(0) == 0)
    def _():
        run_ref[...] = jnp.zeros_like(run_ref)

    h1 = h_ref[...] + _dot(o_ref[...], wo_ref[...])
    h1_ref[...] = h1
    hn = _rmsnorm(h1, g_ref[...])
    hn_ref[...] = hn

    x_hi, x_lo = _split_bf16(hn)
    logits = _dot(x_hi, wr_hi_ref[...]) + (_dot(x_lo, wr_hi_ref[...]) + _dot(x_hi, wr_lo_ref[...]))
    lane = lax.broadcasted_iota(jnp.int32, (rows, LANES), 1)
    logits = jnp.where(lane < n_exp, logits, NEG_BIG)

    def top1(lg):
        m = jnp.max(lg, axis=1, keepdims=True)
        idx = jnp.min(jnp.where(lg == m, lane, LANES), axis=1, keepdims=True)
        return m, idx

    v1, i1 = top1(logits)
    v2, i2 = top1(jnp.where(lane == i1, NEG_BIG, logits))
    e = jnp.exp(v2 - v1)
    g1 = 1.0 / (1.0 + e)
    g2 = e / (1.0 + e)

    oh1 = (lane == i1).astype(F32)
    oh2 = (lane == i2).astype(F32)
    both = oh1 + oh2
    rr = lax.broadcasted_iota(jnp.int32, (rows, rows), 0)
    cc = lax.broadcasted_iota(jnp.int32, (rows, rows), 1)
    earlier = (cc < rr).astype(BF16)
    base = run_ref[...] + _dot(earlier, both.astype(BF16))
    rank1 = jnp.sum(oh1 * base, axis=1, keepdims=True)
    rank2 = jnp.sum(oh2 * base, axis=1, keepdims=True)
    run_ref[...] = run_ref[...] + jnp.sum(both, axis=0, keepdims=True)
    cnt_ref[...] = run_ref[...]

    gate_ref[...] = jnp.where(lane == 0, g1, jnp.where(lane == 1, g2, 0.0))
    packed = jnp.where(lane == 0, i1.astype(F32),
                       jnp.where(lane == 1, i2.astype(F32),
                                 jnp.where(lane == 2, rank1, jnp.where(lane == 3, rank2, 0.0))))
    meta_ref[...] = jnp.transpose(packed)[0:8, :].astype(jnp.int32)


def _wo_router(o, h, wo, g, w_router, n_exp):
    t, d = h.shape
    n_tiles = t // ROW_TILE
    wr = jnp.zeros((d, LANES), F32).at[:, :n_exp].set(w_router)
    wr_hi = wr.astype(BF16)
    wr_lo = (wr - wr_hi.astype(F32)).astype(BF16)
    row = lambda: pl.BlockSpec((ROW_TILE, d), lambda i: (i, 0))
    return pl.pallas_call(
        functools.partial(_wo_router_kernel, n_exp=n_exp),
        out_shape=(jax.ShapeDtypeStruct((t, d), F32),
                   jax.ShapeDtypeStruct((t, d), F32),
                   jax.ShapeDtypeStruct((t, LANES), F32),
                   jax.ShapeDtypeStruct((n_tiles, 8, ROW_TILE), jnp.int32),
                   jax.ShapeDtypeStruct((1, LANES), F32)),
        grid=(n_tiles,),
        in_specs=[row(), row(), _resident((d, d)), _resident((1, d)),
                  _resident((d, LANES)), _resident((d, LANES))],
        out_specs=(row(), row(),
                   pl.BlockSpec((ROW_TILE, LANES), lambda i: (i, 0)),
                   pl.BlockSpec((None, 8, ROW_TILE), lambda i: (i, 0, 0)),
                   pl.BlockSpec((1, LANES), lambda i: (0, 0))),
        scratch_shapes=[pltpu.VMEM((1, LANES), F32)],
        compiler_params=_cparams(("arbitrary",)),
        name="wo_router",
    )(o, h, wo, g.reshape(1, d), wr_hi, wr_lo)


def _row_copy(src, s, dst, d, sem):
    return pltpu.make_async_copy(src.at[pl.ds(s, 1)], dst.at[pl.ds(d, 1)], sem)


def _dispatch_kernel(starts_ref, counts_ref, meta_ref, hn_hbm, xs_hbm, zero_ref, sem, *, n_exp):
    i = pl.program_id(0)
    rows = meta_ref.shape[1]
    zrows = zero_ref.shape[0]

    @pl.when(i == 0)
    def _():
        zero_ref[...] = jnp.zeros_like(zero_ref)
        used = starts_ref[n_exp]
        n_blocks = (xs_hbm.shape[0] - used) // zrows

        def tail_copy(r):
            dst = xs_hbm.at[pl.ds(pl.multiple_of(used + r * zrows, zrows), zrows)]
            return pltpu.make_async_copy(zero_ref, dst, sem)

        def tail_fill(r, c):
            tail_copy(r).start()
            return c

        def tail_drain(r, c):
            tail_copy(r).wait()
            return c

        lax.fori_loop(0, n_blocks, tail_fill, 0)
        lax.fori_loop(0, n_blocks, tail_drain, 0)

        for e in range(n_exp):
            first = starts_ref[e] + counts_ref[e]
            n_pad = starts_ref[e + 1] - first

            def fill(r, c):
                _row_copy(zero_ref, 0, xs_hbm, first + r, sem).start()
                return c

            def drain(r, c):
                _row_copy(zero_ref, 0, xs_hbm, first + r, sem).wait()
                return c

            lax.fori_loop(0, n_pad, fill, 0)
            lax.fori_loop(0, n_pad, drain, 0)

    t0 = i * rows

    def slots(r):
        return (starts_ref[meta_ref[0, r]] + meta_ref[2, r],
                starts_ref[meta_ref[1, r]] + meta_ref[3, r])

    def issue(r, c):
        p1, p2 = slots(r)
        _row_copy(hn_hbm, t0 + r, xs_hbm, p1, sem).start()
        _row_copy(hn_hbm, t0 + r, xs_hbm, p2, sem).start()
        return c

    def drain(r, c):
        p1, p2 = slots(r)
        _row_copy(hn_hbm, t0 + r, xs_hbm, p1, sem).wait()
        _row_copy(hn_hbm, t0 + r, xs_hbm, p2, sem).wait()
        return c

    lax.fori_loop(0, rows, issue, 0)
    lax.fori_loop(0, rows, drain, 0)


def _dispatch(starts, counts, meta, hn, n_sorted, n_exp):
    t, d = hn.shape
    n_tiles = meta.shape[0]
    return pl.pallas_call(
        functools.partial(_dispatch_kernel, n_exp=n_exp),
        out_shape=jax.ShapeDtypeStruct((n_sorted, d), F32),
        grid_spec=pltpu.PrefetchScalarGridSpec(
            num_scalar_prefetch=2,
            grid=(n_tiles,),
            in_specs=[pl.BlockSpec((None, 8, ROW_TILE), lambda i, *_: (i, 0, 0), memory_space=pltpu.SMEM),
                      pl.BlockSpec(memory_space=pl.ANY)],
            out_specs=pl.BlockSpec(memory_space=pl.ANY),
            scratch_shapes=[pltpu.VMEM((8, d), F32), pltpu.SemaphoreType.DMA(())]),
        compiler_params=_cparams(("arbitrary",)),
        name="moe_dispatch",
    )(starts, counts, meta, hn)


def _expert_ffn_kernel(te_ref, nact_ref, xs_ref, wg_ref, wu_ref, wd_ref, y_ref, xb_ref, acc_ref):
    i = pl.program_id(0)
    j = pl.program_id(1)

    @pl.when(i < nact_ref[0])
    def _():
        @pl.when(j == 0)
        def _():
            xb_ref[...] = xs_ref[...].astype(BF16)
            acc_ref[...] = jnp.zeros_like(acc_ref)

        xb = xb_ref[...]
        a = _silu_mul(_dot(xb, wg_ref[...]), _dot(xb, wu_ref[...])).astype(BF16)
        acc_ref[...] += _dot(a, wd_ref[...])

        @pl.when(j == pl.num_programs(1) - 1)
        def _():
            y_ref[...] = acc_ref[...]

    @pl.when((i >= nact_ref[0]) & (j == pl.num_programs(1) - 1))
    def _():
        y_ref[...] = jnp.zeros_like(y_ref)


def _expert_ffn(tile_expert, n_active, xs, wg, wu, wd):
    n_sorted, d = xs.shape
    n_tiles = n_sorted // EXPERT_TILE
    f = wg.shape[2]
    fc = EXPERT_FF_CHUNK if f % EXPERT_FF_CHUNK == 0 else f
    nf = f // fc

    def tile(i, na):
        return jnp.minimum(i, na[0] - 1)

    def fchunk(i, j, na):
        return jnp.where(i < na[0], j, nf - 1)

    return pl.pallas_call(
        _expert_ffn_kernel,
        out_shape=jax.ShapeDtypeStruct((n_sorted, d), F32),
        grid_spec=pltpu.PrefetchScalarGridSpec(
            num_scalar_prefetch=2,
            grid=(n_tiles, nf),
            in_specs=[pl.BlockSpec((EXPERT_TILE, d), lambda i, j, te, na: (tile(i, na), 0)),
                      pl.BlockSpec((None, d, fc), lambda i, j, te, na: (te[i], 0, fchunk(i, j, na))),
                      pl.BlockSpec((None, d, fc), lambda i, j, te, na: (te[i], 0, fchunk(i, j, na))),
                      pl.BlockSpec((None, fc, d), lambda i, j, te, na: (te[i], fchunk(i, j, na), 0))],
            out_specs=pl.BlockSpec((EXPERT_TILE, d), lambda i, j, te, na: (i, 0)),
            scratch_shapes=[pltpu.VMEM((EXPERT_TILE, d), BF16), pltpu.VMEM((EXPERT_TILE, d), F32)]),
        compiler_params=_cparams(("arbitrary", "arbitrary")),
        name="expert_ffn",
    )(tile_expert, n_active, xs, wg, wu, wd)


def _combine_kernel(starts_ref, meta_ref, gate_ref, h1_ref, gfin_ref, y_hbm, out_ref, ya_ref, yb_ref, sem):
    rows = h1_ref.shape[0]

    def slots(r):
        return (starts_ref[meta_ref[0, r]] + meta_ref[2, r],
                starts_ref[meta_ref[1, r]] + meta_ref[3, r])

    def issue(r, c):
        p1, p2 = slots(r)
        _row_copy(y_hbm, p1, ya_ref, r, sem).start()
        _row_copy(y_hbm, p2, yb_ref, r, sem).start()
        return c

    def drain(r, c):
        p1, p2 = slots(r)
        _row_copy(y_hbm, p1, ya_ref, r, sem).wait()
        _row_copy(y_hbm, p2, yb_ref, r, sem).wait()
        return c

    lax.fori_loop(0, rows, issue, 0)
    lax.fori_loop(0, rows, drain, 0)
    gates = gate_ref[...]
    h2 = h1_ref[...] + (gates[:, 0:1] * ya_ref[...] + gates[:, 1:2] * yb_ref[...])
    out_ref[...] = _rmsnorm(h2, gfin_ref[...])


def _combine(starts, meta, gates, h1, gfin, y):
    t, d = h1.shape
    n_tiles = meta.shape[0]
    return pl.pallas_call(
        _combine_kernel,
        out_shape=jax.ShapeDtypeStruct((t, d), F32),
        grid_spec=pltpu.PrefetchScalarGridSpec(
            num_scalar_prefetch=1,
            grid=(n_tiles,),
            in_specs=[pl.BlockSpec((None, 8, ROW_TILE), lambda i, *_: (i, 0, 0), memory_space=pltpu.SMEM),
                      pl.BlockSpec((ROW_TILE, LANES), lambda i, *_: (i, 0)),
                      pl.BlockSpec((ROW_TILE, d), lambda i, *_: (i, 0)),
                      pl.BlockSpec((1, d), lambda i, *_: (0, 0)),
                      pl.BlockSpec(memory_space=pl.ANY)],
            out_specs=pl.BlockSpec((ROW_TILE, d), lambda i, *_: (i, 0)),
            scratch_shapes=[pltpu.VMEM((ROW_TILE, d), F32), pltpu.VMEM((ROW_TILE, d), F32),
                            pltpu.SemaphoreType.DMA(())]),
        compiler_params=_cparams(("arbitrary",)),
        name="moe_combine",
    )(starts, meta, gates, h1, gfin.reshape(1, d), y)


def kernel(x, meta_tokens, norm_gains, sb_w_qkv, sb_w_o, diff_w_qkv, diff_w_o, diff_subln,
           diff_lambda_q1, diff_lambda_k1, diff_lambda_q2, diff_lambda_k2,
           ffn_w_gate, ffn_w_up, ffn_w_down, moe_router, moe_w_gate, moe_w_up, moe_w_down,
           final_norm):
    batch, seq_in, d = x.shape
    n_meta = meta_tokens.shape[0]
    seq = n_meta + seq_in
    t = batch * seq
    n_exp = moe_router.shape[-1]
    assert d % LANES == 0 and t % ROW_TILE == 0 and seq % 16 == 0 and seq >= QB
    assert diff_subln.shape[-1] == LANES and n_exp <= 8

    meta = jnp.broadcast_to(meta_tokens.astype(x.dtype)[None], (batch, n_meta, d))
    h = jnp.concatenate([meta, x], axis=1).reshape(t, d)

    qkv = _norm_matmul(h, norm_gains[0, 0], sb_w_qkv[0].astype(BF16))
    o = _sb_attention(qkv, batch, seq, d)
    h = _wo_ffn(o, h, sb_w_o[0].astype(BF16), norm_gains[0, 1],
                ffn_w_gate[0].astype(BF16), ffn_w_up[0].astype(BF16), ffn_w_down[0].astype(BF16))

    lambda_init = 0.8 - 0.6 * math.exp(-0.3 * 1)
    qkv = _norm_matmul(h, norm_gains[1, 0], diff_w_qkv[0].astype(BF16))
    lam_params = jnp.stack([diff_lambda_q1[0], diff_lambda_k1[0], diff_lambda_q2[0], diff_lambda_k2[0]])
    o = _diff_attention(qkv, lam_params, diff_subln[0], batch, seq, d, lambda_init)
    h1, hn, gates, route, counts = _wo_router(o, h, diff_w_o[0].astype(BF16), norm_gains[1, 1],
                                              moe_router[0], n_exp)

    counts = counts[0, :n_exp].astype(jnp.int32)
    tiles_per = (counts + EXPERT_TILE - 1) // EXPERT_TILE
    tile_end = jnp.cumsum(tiles_per)
    starts = jnp.concatenate([jnp.zeros((1,), jnp.int32), tile_end * EXPERT_TILE]).astype(jnp.int32)
    n_tiles = (TOP_K_SLOTS * t + EXPERT_TILE - 1) // EXPERT_TILE + n_exp
    n_active = tile_end[-1:].astype(jnp.int32)
    tile_ids = jnp.minimum(jnp.arange(n_tiles, dtype=jnp.int32), n_active[0] - 1)
    tile_expert = jnp.sum((tile_ids[:, None] >= tile_end[None, :]).astype(jnp.int32), axis=1)

    xs = _dispatch(starts, counts, route, hn, n_tiles * EXPERT_TILE, n_exp)
    y = _expert_ffn(tile_expert, n_active, xs, moe_w_gate[0].astype(BF16), moe_w_up[0].astype(BF16),
                    moe_w_down[0].astype(BF16))
    out = _combine(starts, route, gates, h1, final_norm, y)
    return out.reshape(batch, seq, d)[:, n_meta:]


TOP_K_SLOTS = 2
```

```python
import functools
import math

import jax
import jax.numpy as jnp
from jax import lax
from jax.experimental import pallas as pl
from jax.experimental.pallas import tpu as pltpu

F32 = jnp.float32
BF16 = jnp.bfloat16

NORM_EPS = 1e-6
ROPE_THETA = 10000.0
HEAD_DIM = 64
LANES = 128
QB = 128
TQ = 512
TK = 256
SB_GROUP = 2
ATTN_LANES = 512
ROW_TILE = 512
EXPERT_TILE = 1024
EXPERT_FF_CHUNK = 512
FF_CHUNK = 256
DMA_UNROLL = 8
TOP_K_SLOTS = 2
NEG_BIG = -1e30
LOG2E = 1.4426950408889634
VMEM_LIMIT = 56 * 1024 * 1024


def _cparams(sem):
    return pltpu.CompilerParams(dimension_semantics=sem, vmem_limit_bytes=VMEM_LIMIT)


def _rmsnorm(x, g):
    ms = jnp.mean(x * x, axis=-1, keepdims=True)
    return x * lax.rsqrt(ms + NORM_EPS) * g


def _dot(a, b):
    return jnp.dot(a, b, preferred_element_type=F32)


def _dot_t(a, b):
    return lax.dot_general(a, b, (((1,), (1,)), ((), ())), preferred_element_type=F32)


def _split_bf16(x):
    hi = x.astype(BF16)
    lo = (x - hi.astype(F32)).astype(BF16)
    return hi, lo


def _resident(shape):
    nd = len(shape)
    return pl.BlockSpec(shape, lambda *_: (0,) * nd, pipeline_mode=pl.Buffered(1))


def _norm_matmul_kernel(h_ref, g_ref, w_ref, o_ref, *, n_chunk):
    hn = _rmsnorm(h_ref[...], g_ref[...]).astype(BF16)
    for c in range(0, w_ref.shape[1], n_chunk):
        o_ref[:, c:c + n_chunk] = _dot(hn, w_ref[:, c:c + n_chunk]).astype(o_ref.dtype)


def _norm_matmul(h, g, w):
    t, d = h.shape
    n = w.shape[1]
    return pl.pallas_call(
        functools.partial(_norm_matmul_kernel, n_chunk=min(n, 1024)),
        out_shape=jax.ShapeDtypeStruct((t, n), BF16),
        grid=(t // ROW_TILE,),
        in_specs=[pl.BlockSpec((ROW_TILE, d), lambda i: (i, 0)),
                  _resident((1, d)),
                  _resident((d, n))],
        out_specs=pl.BlockSpec((ROW_TILE, n), lambda i: (i, 0)),
        compiler_params=_cparams(("parallel",)),
        name="norm_qkv",
    )(h, g.reshape(1, d), w)


def _stacked_idx(rows, width):
    rr = lax.broadcasted_iota(jnp.int32, (2 * rows, width), 0)
    cc = lax.broadcasted_iota(jnp.int32, (2 * rows, width), 1)
    return jnp.where(rr >= rows, rr - rows, rr), cc


def _sb_attn_kernel(q_ref, k_ref, v_ref, o_ref, acc_ref, *, seq, scale):
    n_pairs = q_ref.shape[1] // LANES
    nq = seq // TQ
    tail = seq - nq * TQ
    low_lanes = lax.broadcasted_iota(jnp.int32, (1, LANES), 1) < HEAD_DIM
    j_loc, s_loc = _stacked_idx(QB, QB)
    from_s = (j_loc >= s_loc).astype(BF16)

    def cols(pr):
        return slice(pr * LANES, (pr + 1) * LANES)

    def stack_heads(x):
        zero = jnp.zeros_like(x)
        return jnp.concatenate([jnp.where(low_lanes, x, zero), jnp.where(low_lanes, zero, x)], axis=0)

    def chunk(qzs, k0, cs, rows, mask):
        new_cs = []
        for g0 in range(0, n_pairs, SB_GROUP):
            group = range(g0, min(g0 + SB_GROUP, n_pairs))
            neg_zs, log_keeps = [], []
            for pr in group:
                nz = _dot_t(qzs[pr], k_ref[pl.ds(k0, TK), cols(pr)])
                soft = jnp.log(1.0 + jnp.exp2(jnp.abs(nz) * (-LOG2E)))
                log_keep = jnp.minimum(nz, 0.0) - soft
                if mask is not None:
                    log_keep = jnp.where(mask, log_keep, 0.0)
                neg_zs.append(nz)
                log_keeps.append(log_keep)
            halves = []
            for h0 in range(0, TK, QB):
                parts = []
                for lk in log_keeps:
                    hi, lo = _split_bf16(lk[:, h0:h0 + QB])
                    parts.append(jnp.concatenate([hi, lo], axis=1))
                halves.append(_dot(jnp.concatenate(parts, axis=0), from_s))
            for n, pr in enumerate(group):
                run = None if cs is None else cs[pr]
                ts = []
                for hh in reversed(range(TK // QB)):
                    keep_from = halves[hh][n * 2 * rows:(n + 1) * 2 * rows]
                    t = keep_from - neg_zs[n][:, hh * QB:(hh + 1) * QB]
                    if run is not None:
                        t = t + run
                    ts.append(t)
                    half_sum = keep_from[:, 0:1]
                    run = half_sum if run is None else run + half_sum
                p = jnp.exp2(jnp.concatenate(ts[::-1], axis=1) * LOG2E)
                if mask is not None:
                    p = jnp.where(mask, p, 0.0)
                pb = p.astype(BF16)
                contrib = _dot(jnp.concatenate([pb[:rows], pb[rows:]], axis=1),
                               stack_heads(v_ref[pl.ds(k0, TK), cols(pr)]))
                if cs is None:
                    acc_ref[0:rows, cols(pr)] = contrib
                else:
                    acc_ref[0:rows, cols(pr)] += contrib
                new_cs.append(run)
        return tuple(new_cs)

    def q_block(q0, rows, masked, n_below):
        qzs = []
        for pr in range(n_pairs):
            qs = (q_ref[pl.ds(q0, rows), cols(pr)].astype(F32) * (-scale)).astype(BF16)
            qzs.append(stack_heads(qs))
        cs = None
        for k0, mask in masked:
            cs = chunk(qzs, k0, cs, rows, mask)

        def below(idx, cs):
            return chunk(qzs, pl.multiple_of((n_below - 1 - idx) * TK, TK), cs, rows, None)

        lax.fori_loop(0, n_below, below, cs)
        o_ref[pl.ds(q0, rows), :] = acc_ref[0:rows, :].astype(o_ref.dtype)

    def q_loop(i, carry):
        q0 = pl.multiple_of(i * TQ, TQ)
        t_loc, s_loc = _stacked_idx(TQ, TK)
        masked = []
        for m in reversed(range(TQ // TK)):
            masked.append((q0 + m * TK, s_loc + m * TK < t_loc))
        q_block(q0, TQ, masked, i * (TQ // TK))
        return carry

    lax.fori_loop(0, nq, q_loop, 0)

    if tail:
        q0 = nq * TQ
        k0 = seq - TK
        t_loc, s_loc = _stacked_idx(tail, TK)
        q_block(q0, tail, [(k0, (s_loc + k0 >= q0) & (s_loc + k0 < t_loc + q0))], q0 // TK)


def _attn_specs(seq, d_model):
    nblk = d_model // ATTN_LANES
    spec = lambda off: pl.BlockSpec((None, seq, ATTN_LANES), lambda b, c: (b, 0, off + c))
    return nblk, spec(0), spec(nblk), spec(2 * nblk)


def _sb_attention(qkv, batch, seq, d_model):
    qkv3 = qkv.reshape(batch, seq, 3 * d_model)
    nblk, q_spec, k_spec, v_spec = _attn_specs(seq, d_model)
    out = pl.pallas_call(
        functools.partial(_sb_attn_kernel, seq=seq, scale=HEAD_DIM ** -0.5),
        out_shape=jax.ShapeDtypeStruct((batch, seq, d_model), BF16),
        grid=(batch, nblk),
        in_specs=[q_spec, k_spec, v_spec],
        out_specs=q_spec,
        scratch_shapes=[pltpu.VMEM((TQ, ATTN_LANES), F32)],
        compiler_params=_cparams(("parallel", "parallel")),
        name="sb_attention",
    )(qkv3, qkv3, qkv3)
    return out.reshape(batch * seq, d_model)


def _diff_attn_kernel(q_ref, k_ref, v_ref, cos_ref, sin_ref, lam_ref, subg_ref, o_ref, krot_ref, acc_ref,
                      *, seq, scale, lambda_init):
    n_heads = q_ref.shape[1] // LANES
    nq = seq // TQ
    tail = seq - nq * TQ
    lane = lax.broadcasted_iota(jnp.int32, (1, LANES), 1)
    low_lanes = lane < HEAD_DIM
    first_half = (lane & (HEAD_DIM // 2)) == 0

    lam_p = lam_ref[...]
    lam = (jnp.exp(jnp.sum(lam_p[0:1] * lam_p[1:2], axis=-1, keepdims=True))
           - jnp.exp(jnp.sum(lam_p[2:3] * lam_p[3:4], axis=-1, keepdims=True))
           + lambda_init)

    def cols(h):
        return slice(h * LANES, (h + 1) * LANES)

    def rope(x, r0, rows):
        partner = jnp.where(first_half, pltpu.roll(x, LANES - HEAD_DIM // 2, 1),
                            pltpu.roll(x, HEAD_DIM // 2, 1))
        return x * cos_ref[pl.ds(r0, rows), :] + partner * sin_ref[pl.ds(r0, rows), :]

    def rot_keys(r0, rows):
        for h in range(n_heads):
            kh = k_ref[pl.ds(r0, rows), cols(h)].astype(F32)
            krot_ref[pl.ds(r0, rows), cols(h)] = rope(kh, r0, rows).astype(BF16)

    def rot_loop(i, carry):
        rot_keys(pl.multiple_of(i * QB, QB), QB)
        return carry

    lax.fori_loop(0, seq // QB, rot_loop, 0)
    if seq % QB:
        rot_keys(seq - seq % QB, seq % QB)

    def stack_maps(x):
        zero = jnp.zeros_like(x)
        return jnp.concatenate([jnp.where(low_lanes, x, zero), jnp.where(low_lanes, zero, x)], axis=0)

    def chunk(qzs, k0, st, rows, mask):
        out = []
        for h in range(n_heads):
            s = _dot_t(qzs[h], krot_ref[pl.ds(k0, TK), cols(h)])
            if mask is not None:
                s = jnp.where(mask, s, NEG_BIG)
            mx = jnp.max(s, axis=1, keepdims=True)
            vb = v_ref[pl.ds(k0, TK), cols(h)]
            if st is None:
                p = jnp.exp(s - mx)
                acc_ref[h, 0:2 * rows, :] = _dot(p.astype(BF16), vb)
                out.extend([mx, jnp.sum(p, axis=1, keepdims=True)])
            else:
                m, l = st[2 * h], st[2 * h + 1]
                m_new = jnp.maximum(m, mx)
                alpha = jnp.exp(m - m_new)
                p = jnp.exp(s - m_new)
                acc_ref[h, 0:2 * rows, :] = alpha * acc_ref[h, 0:2 * rows, :] + _dot(p.astype(BF16), vb)
                out.extend([m_new, alpha * l + jnp.sum(p, axis=1, keepdims=True)])
        return tuple(out)

    def q_block(q0, rows, masked, n_below):
        qzs = []
        for h in range(n_heads):
            qh = rope(q_ref[pl.ds(q0, rows), cols(h)].astype(F32), q0, rows) * scale
            qzs.append(stack_maps(qh.astype(BF16)))
        st = None
        for k0, mask in masked:
            st = chunk(qzs, k0, st, rows, mask)

        def below(idx, st):
            return chunk(qzs, pl.multiple_of(idx * TK, TK), st, rows, None)

        st = lax.fori_loop(0, n_below, below, st)
        for h in range(n_heads):
            a = acc_ref[h, 0:2 * rows, :] / st[2 * h + 1]
            o = a[:rows] - lam * a[rows:]
            o = _rmsnorm(o, subg_ref[...]) * (1.0 - lambda_init)
            o_ref[pl.ds(q0, rows), cols(h)] = o.astype(o_ref.dtype)

    def q_loop(i, carry):
        q0 = pl.multiple_of(i * TQ, TQ)
        t_loc, s_loc = _stacked_idx(TQ, TK)
        masked = [(q0 + m * TK, s_loc + m * TK <= t_loc) for m in range(TQ // TK)]
        q_block(q0, TQ, masked, i * (TQ // TK))
        return carry

    lax.fori_loop(0, nq, q_loop, 0)

    if tail:
        q0 = nq * TQ
        k0 = seq - TK
        t_loc, s_loc = _stacked_idx(tail, TK)
        q_block(q0, tail, [(k0, (s_loc + k0 >= q0) & (s_loc + k0 <= t_loc + q0))], q0 // TK)


def _rope_tables(seq):
    inv = ROPE_THETA ** (-jnp.arange(0, HEAD_DIM, 2, dtype=F32) / HEAD_DIM)
    ang = jnp.arange(seq, dtype=F32)[:, None] * inv[None, :]
    cos, sin = jnp.cos(ang), jnp.sin(ang)
    reps = LANES // HEAD_DIM
    cos_t = jnp.tile(cos, (1, 2 * reps))
    sin_t = jnp.tile(jnp.concatenate([-sin, sin], axis=1), (1, reps))
    assert cos_t.shape == (seq, LANES) and sin_t.shape == (seq, LANES)
    return cos_t, sin_t


def _diff_attention(qkv, lam_params, subln_g, batch, seq, d_model, lambda_init):
    qkv3 = qkv.reshape(batch, seq, 3 * d_model)
    cos_t, sin_t = _rope_tables(seq)
    nblk, q_spec, k_spec, v_spec = _attn_specs(seq, d_model)
    out = pl.pallas_call(
        functools.partial(_diff_attn_kernel, seq=seq, scale=HEAD_DIM ** -0.5, lambda_init=lambda_init),
        out_shape=jax.ShapeDtypeStruct((batch, seq, d_model), BF16),
        grid=(batch, nblk),
        in_specs=[q_spec, k_spec, v_spec,
                  _resident((seq, LANES)), _resident((seq, LANES)),
                  _resident(lam_params.shape), _resident((1, LANES))],
        out_specs=q_spec,
        scratch_shapes=[pltpu.VMEM((seq, ATTN_LANES), BF16),
                        pltpu.VMEM((ATTN_LANES // LANES, 2 * TQ, LANES), F32)],
        compiler_params=_cparams(("parallel", "parallel")),
        name="diff_attention",
    )(qkv3, qkv3, qkv3, cos_t, sin_t, lam_params, subln_g.reshape(1, LANES))
    return out.reshape(batch * seq, d_model)


def _silu_mul(g, u):
    return g * (1.0 / (1.0 + jnp.exp(-g))) * u


def _wo_ffn_kernel(o_ref, h_ref, wo_ref, g_ref, wg_ref, wu_ref, wd_ref, out_ref, a_ref, *, f_chunk):
    h1 = h_ref[...] + _dot(o_ref[...], wo_ref[...])
    hn = _rmsnorm(h1, g_ref[...]).astype(BF16)
    for c in range(0, wg_ref.shape[1], f_chunk):
        gate = _dot(hn, wg_ref[:, c:c + f_chunk])
        up = _dot(hn, wu_ref[:, c:c + f_chunk])
        a_ref[:, c:c + f_chunk] = _silu_mul(gate, up).astype(BF16)
    out_ref[...] = h1 + _dot(a_ref[...], wd_ref[...])


def _wo_ffn(o, h, wo, g, wg, wu, wd):
    t, d = h.shape
    f = wg.shape[1]
    f_chunk = FF_CHUNK if f % FF_CHUNK == 0 else f
    row = lambda: pl.BlockSpec((ROW_TILE, d), lambda i: (i, 0))
    return pl.pallas_call(
        functools.partial(_wo_ffn_kernel, f_chunk=f_chunk),
        out_shape=jax.ShapeDtypeStruct((t, d), F32),
        grid=(t // ROW_TILE,),
        in_specs=[row(), row(), _resident((d, d)), _resident((1, d)),
                  _resident((d, f)), _resident((d, f)), _resident((f, d))],
        out_specs=row(),
        scratch_shapes=[pltpu.VMEM((ROW_TILE, f), BF16)],
        compiler_params=_cparams(("parallel",)),
        name="wo_ffn",
    )(o, h, wo, g.reshape(1, d), wg, wu, wd)


def _wo_router_kernel(o_ref, h_ref, wo_ref, g_ref, wr_hi_ref, wr_lo_ref,
                      h1_ref, hn_ref, gate_ref, meta_ref, cnt_ref, run_ref, *, n_exp):
    rows = h_ref.shape[0]

    @pl.when(pl.program_id(0) == 0)
    def _():
        run_ref[...] = jnp.zeros_like(run_ref)

    h1 = h_ref[...] + _dot(o_ref[...], wo_ref[...])
    h1_ref[...] = h1
    hn = _rmsnorm(h1, g_ref[...])
    hn_ref[...] = hn

    x_hi, x_lo = _split_bf16(hn)
    logits = _dot(x_hi, wr_hi_ref[...]) + (_dot(x_lo, wr_hi_ref[...]) + _dot(x_hi, wr_lo_ref[...]))
    lane = lax.broadcasted_iota(jnp.int32, (rows, LANES), 1)
    logits = jnp.where(lane < n_exp, logits, NEG_BIG)

    def top1(lg):
        m = jnp.max(lg, axis=1, keepdims=True)
        idx = jnp.min(jnp.where(lg == m, lane, LANES), axis=1, keepdims=True)
        return m, idx

    v1, i1 = top1(logits)
    v2, i2 = top1(jnp.where(lane == i1, NEG_BIG, logits))
    e = jnp.exp(v2 - v1)
    g1 = 1.0 / (1.0 + e)
    g2 = e / (1.0 + e)

    oh1 = (lane == i1).astype(F32)
    oh2 = (lane == i2).astype(F32)
    both = oh1 + oh2
    rr = lax.broadcasted_iota(jnp.int32, (rows, rows), 0)
    cc = lax.broadcasted_iota(jnp.int32, (rows, rows), 1)
    earlier = (cc < rr).astype(BF16)
    base = run_ref[...] + _dot(earlier, both.astype(BF16))
    rank1 = jnp.sum(oh1 * base, axis=1, keepdims=True)
    rank2 = jnp.sum(oh2 * base, axis=1, keepdims=True)
    run_ref[...] = run_ref[...] + jnp.sum(both, axis=0, keepdims=True)
    cnt_ref[...] = run_ref[...]

    gate_ref[...] = jnp.where(lane == 0, g1, jnp.where(lane == 1, g2, 0.0))
    packed = jnp.where(lane == 0, i1.astype(F32),
                       jnp.where(lane == 1, i2.astype(F32),
                                 jnp.where(lane == 2, rank1, jnp.where(lane == 3, rank2, 0.0))))
    meta_ref[...] = jnp.transpose(packed)[0:8, :].astype(jnp.int32)


def _wo_router(o, h, wo, g, w_router, n_exp):
    t, d = h.shape
    n_tiles = t // ROW_TILE
    wr = jnp.zeros((d, LANES), F32).at[:, :n_exp].set(w_router)
    wr_hi = wr.astype(BF16)
    wr_lo = (wr - wr_hi.astype(F32)).astype(BF16)
    row = lambda: pl.BlockSpec((ROW_TILE, d), lambda i: (i, 0))
    return pl.pallas_call(
        functools.partial(_wo_router_kernel, n_exp=n_exp),
        out_shape=(jax.ShapeDtypeStruct((t, d), F32),
                   jax.ShapeDtypeStruct((t, d), F32),
                   jax.ShapeDtypeStruct((t, LANES), F32),
                   jax.ShapeDtypeStruct((n_tiles, 8, ROW_TILE), jnp.int32),
                   jax.ShapeDtypeStruct((1, LANES), F32)),
        grid=(n_tiles,),
        in_specs=[row(), row(), _resident((d, d)), _resident((1, d)),
                  _resident((d, LANES)), _resident((d, LANES))],
        out_specs=(row(), row(),
                   pl.BlockSpec((ROW_TILE, LANES), lambda i: (i, 0)),
                   pl.BlockSpec((None, 8, ROW_TILE), lambda i: (i, 0, 0)),
                   pl.BlockSpec((1, LANES), lambda i: (0, 0))),
        scratch_shapes=[pltpu.VMEM((1, LANES), F32)],
        compiler_params=_cparams(("arbitrary",)),
        name="wo_router",
    )(o, h, wo, g.reshape(1, d), wr_hi, wr_lo)


def _row_copy(src, s, dst, d, sem):
    return pltpu.make_async_copy(src.at[pl.ds(s, 1)], dst.at[pl.ds(d, 1)], sem)


def _dispatch_kernel(starts_ref, counts_ref, meta_ref, hn_ref, xs_hbm, zero_ref, sem, *, n_exp):
    i = pl.program_id(0)
    rows = meta_ref.shape[1]
    zrows = zero_ref.shape[0]

    @pl.when(i == 0)
    def _():
        zero_ref[...] = jnp.zeros_like(zero_ref)
        used = starts_ref[n_exp]
        n_blocks = (xs_hbm.shape[0] - used) // zrows

        def tail_copy(r):
            dst = xs_hbm.at[pl.ds(pl.multiple_of(used + r * zrows, zrows), zrows)]
            return pltpu.make_async_copy(zero_ref, dst, sem)

        def tail_fill(r, c):
            tail_copy(r).start()
            return c

        def tail_drain(r, c):
            tail_copy(r).wait()
            return c

        lax.fori_loop(0, n_blocks, tail_fill, 0)
        lax.fori_loop(0, n_blocks, tail_drain, 0)

        for e in range(n_exp):
            first = starts_ref[e] + counts_ref[e]
            n_pad = starts_ref[e + 1] - first

            def fill(r, c):
                _row_copy(zero_ref, 0, xs_hbm, first + r, sem).start()
                return c

            def drain(r, c):
                _row_copy(zero_ref, 0, xs_hbm, first + r, sem).wait()
                return c

            lax.fori_loop(0, n_pad, fill, 0)
            lax.fori_loop(0, n_pad, drain, 0)

    def slots(r):
        return (starts_ref[meta_ref[0, r]] + meta_ref[2, r],
                starts_ref[meta_ref[1, r]] + meta_ref[3, r])

    def issue(r, c):
        p1, p2 = slots(r)
        _row_copy(hn_ref, r, xs_hbm, p1, sem).start()
        _row_copy(hn_ref, r, xs_hbm, p2, sem).start()
        return c

    def drain(r, c):
        p1, p2 = slots(r)
        _row_copy(hn_ref, r, xs_hbm, p1, sem).wait()
        _row_copy(hn_ref, r, xs_hbm, p2, sem).wait()
        return c

    lax.fori_loop(0, rows, issue, 0, unroll=DMA_UNROLL)
    lax.fori_loop(0, rows, drain, 0, unroll=DMA_UNROLL)


def _dispatch(starts, counts, meta, hn, n_sorted, n_exp):
    t, d = hn.shape
    n_tiles = meta.shape[0]
    return pl.pallas_call(
        functools.partial(_dispatch_kernel, n_exp=n_exp),
        out_shape=jax.ShapeDtypeStruct((n_sorted, d), F32),
        grid_spec=pltpu.PrefetchScalarGridSpec(
            num_scalar_prefetch=2,
            grid=(n_tiles,),
            in_specs=[pl.BlockSpec((None, 8, ROW_TILE), lambda i, *_: (i, 0, 0), memory_space=pltpu.SMEM),
                      pl.BlockSpec((ROW_TILE, d), lambda i, *_: (i, 0))],
            out_specs=pl.BlockSpec(memory_space=pl.ANY),
            scratch_shapes=[pltpu.VMEM((8, d), F32), pltpu.SemaphoreType.DMA(())]),
        compiler_params=_cparams(("arbitrary",)),
        name="moe_dispatch",
    )(starts, counts, meta, hn)


def _expert_ffn_kernel(te_ref, nact_ref, xs_ref, wg_ref, wu_ref, wd_ref, y_ref, xb_ref, acc_ref):
    i = pl.program_id(0)
    j = pl.program_id(1)

    @pl.when(i < nact_ref[0])
    def _():
        @pl.when(j == 0)
        def _():
            xb_ref[...] = xs_ref[...].astype(BF16)
            acc_ref[...] = jnp.zeros_like(acc_ref)

        xb = xb_ref[...]
        a = _silu_mul(_dot(xb, wg_ref[...]), _dot(xb, wu_ref[...])).astype(BF16)
        acc_ref[...] += _dot(a, wd_ref[...])

        @pl.when(j == pl.num_programs(1) - 1)
        def _():
            y_ref[...] = acc_ref[...]

    @pl.when((i >= nact_ref[0]) & (j == pl.num_programs(1) - 1))
    def _():
        y_ref[...] = jnp.zeros_like(y_ref)


def _expert_ffn(tile_expert, n_active, xs, wg, wu, wd):
    n_sorted, d = xs.shape
    n_tiles = n_sorted // EXPERT_TILE
    f = wg.shape[2]
    fc = EXPERT_FF_CHUNK if f % EXPERT_FF_CHUNK == 0 else f
    nf = f // fc

    def tile(i, na):
        return jnp.minimum(i, na[0] - 1)

    def fchunk(i, j, na):
        return jnp.where(i < na[0], j, nf - 1)

    return pl.pallas_call(
        _expert_ffn_kernel,
        out_shape=jax.ShapeDtypeStruct((n_sorted, d), F32),
        grid_spec=pltpu.PrefetchScalarGridSpec(
            num_scalar_prefetch=2,
            grid=(n_tiles, nf),
            in_specs=[pl.BlockSpec((EXPERT_TILE, d), lambda i, j, te, na: (tile(i, na), 0)),
                      pl.BlockSpec((None, d, fc), lambda i, j, te, na: (te[i], 0, fchunk(i, j, na))),
                      pl.BlockSpec((None, d, fc), lambda i, j, te, na: (te[i], 0, fchunk(i, j, na))),
                      pl.BlockSpec((None, fc, d), lambda i, j, te, na: (te[i], fchunk(i, j, na), 0))],
            out_specs=pl.BlockSpec((EXPERT_TILE, d), lambda i, j, te, na: (i, 0)),
            scratch_shapes=[pltpu.VMEM((EXPERT_TILE, d), BF16), pltpu.VMEM((EXPERT_TILE, d), F32)]),
        compiler_params=_cparams(("arbitrary", "arbitrary")),
        name="expert_ffn",
    )(tile_expert, n_active, xs, wg, wu, wd)


def _combine_kernel(starts_ref, meta_ref, gate_ref, h1_ref, gfin_ref, y_hbm, out_ref, ya_ref, yb_ref, sem):
    rows = h1_ref.shape[0]

    def slots(r):
        return (starts_ref[meta_ref[0, r]] + meta_ref[2, r],
                starts_ref[meta_ref[1, r]] + meta_ref[3, r])

    def issue(r, c):
        p1, p2 = slots(r)
        _row_copy(y_hbm, p1, ya_ref, r, sem).start()
        _row_copy(y_hbm, p2, yb_ref, r, sem).start()
        return c

    def drain(r, c):
        p1, p2 = slots(r)
        _row_copy(y_hbm, p1, ya_ref, r, sem).wait()
        _row_copy(y_hbm, p2, yb_ref, r, sem).wait()
        return c

    lax.fori_loop(0, rows, issue, 0, unroll=DMA_UNROLL)
    lax.fori_loop(0, rows, drain, 0, unroll=DMA_UNROLL)
    gates = gate_ref[...]
    h2 = h1_ref[...] + (gates[:, 0:1] * ya_ref[...] + gates[:, 1:2] * yb_ref[...])
    out_ref[...] = _rmsnorm(h2, gfin_ref[...])


def _combine(starts, meta, gates, h1, gfin, y):
    t, d = h1.shape
    n_tiles = meta.shape[0]
    return pl.pallas_call(
        _combine_kernel,
        out_shape=jax.ShapeDtypeStruct((t, d), F32),
        grid_spec=pltpu.PrefetchScalarGridSpec(
            num_scalar_prefetch=1,
            grid=(n_tiles,),
            in_specs=[pl.BlockSpec((None, 8, ROW_TILE), lambda i, *_: (i, 0, 0), memory_space=pltpu.SMEM),
                      pl.BlockSpec((ROW_TILE, LANES), lambda i, *_: (i, 0)),
                      pl.BlockSpec((ROW_TILE, d), lambda i, *_: (i, 0)),
                      pl.BlockSpec((1, d), lambda i, *_: (0, 0)),
                      pl.BlockSpec(memory_space=pl.ANY)],
            out_specs=pl.BlockSpec((ROW_TILE, d), lambda i, *_: (i, 0)),
            scratch_shapes=[pltpu.VMEM((ROW_TILE, d), F32), pltpu.VMEM((ROW_TILE, d), F32),
                            pltpu.SemaphoreType.DMA(())]),
        compiler_params=_cparams(("arbitrary",)),
        name="moe_combine",
    )(starts, meta, gates, h1, gfin.reshape(1, d), y)


def kernel(x, meta_tokens, norm_gains, sb_w_qkv, sb_w_o, diff_w_qkv, diff_w_o, diff_subln,
           diff_lambda_q1, diff_lambda_k1, diff_lambda_q2, diff_lambda_k2,
           ffn_w_gate, ffn_w_up, ffn_w_down, moe_router, moe_w_gate, moe_w_up, moe_w_down,
           final_norm):
    batch, seq_in, d = x.shape
    n_meta = meta_tokens.shape[0]
    seq = n_meta + seq_in
    t = batch * seq
    n_exp = moe_router.shape[-1]
    assert d % LANES == 0 and t % ROW_TILE == 0 and seq % 16 == 0 and seq >= TQ and seq % TQ < TK
    assert diff_subln.shape[-1] == LANES and n_exp <= 8

    meta = jnp.broadcast_to(meta_tokens.astype(x.dtype)[None], (batch, n_meta, d))
    h = jnp.concatenate([meta, x], axis=1).reshape(t, d)

    qkv = _norm_matmul(h, norm_gains[0, 0], sb_w_qkv[0].astype(BF16))
    o = _sb_attention(qkv, batch, seq, d)
    h = _wo_ffn(o, h, sb_w_o[0].astype(BF16), norm_gains[0, 1],
                ffn_w_gate[0].astype(BF16), ffn_w_up[0].astype(BF16), ffn_w_down[0].astype(BF16))

    lambda_init = 0.8 - 0.6 * math.exp(-0.3 * 1)
    qkv = _norm_matmul(h, norm_gains[1, 0], diff_w_qkv[0].astype(BF16))
    lam_params = jnp.stack([diff_lambda_q1[0], diff_lambda_k1[0], diff_lambda_q2[0], diff_lambda_k2[0]])
    o = _diff_attention(qkv, lam_params, diff_subln[0], batch, seq, d, lambda_init)
    h1, hn, gates, route, counts = _wo_router(o, h, diff_w_o[0].astype(BF16), norm_gains[1, 1],
                                              moe_router[0], n_exp)

    counts = counts[0, :n_exp].astype(jnp.int32)
    tiles_per = (counts + EXPERT_TILE - 1) // EXPERT_TILE
    tile_end = jnp.cumsum(tiles_per)
    starts = jnp.concatenate([jnp.zeros((1,), jnp.int32), tile_end * EXPERT_TILE]).astype(jnp.int32)
    n_tiles = (TOP_K_SLOTS * t + EXPERT_TILE - 1) // EXPERT_TILE + n_exp
    n_active = tile_end[-1:].astype(jnp.int32)
    tile_ids = jnp.minimum(jnp.arange(n_tiles, dtype=jnp.int32), n_active[0] - 1)
    tile_expert = jnp.sum((tile_ids[:, None] >= tile_end[None, :]).astype(jnp.int32), axis=1)

    xs = _dispatch(starts, counts, route, hn, n_tiles * EXPERT_TILE, n_exp)
    y = _expert_ffn(tile_expert, n_active, xs, moe_w_gate[0].astype(BF16), moe_w_up[0].astype(BF16),
                    moe_w_down[0].astype(BF16))
    out = _combine(starts, route, gates, h1, final_norm, y)
    return out.reshape(batch, seq, d)[:, n_meta:]
```

```python
import functools
import math

import jax
import jax.numpy as jnp
from jax import lax
from jax.experimental import pallas as pl
from jax.experimental.pallas import tpu as pltpu

F32 = jnp.float32
BF16 = jnp.bfloat16

NORM_EPS = 1e-6
ROPE_THETA = 10000.0
HEAD_DIM = 64
LANES = 128
QB = 128
TQ = 512
SB_TQ = 256
TK = 256
DIFF_TK = 512
SB_GROUP = 2
ATTN_LANES = 512
ROW_TILE = 512
EXPERT_TILE = 1024
EXPERT_FF_CHUNK = 1792
FF_CHUNK = 256
DMA_UNROLL = 8
TOP_K_SLOTS = 2
NEG_BIG = -1e30
LOG2E = 1.4426950408889634
VMEM_LIMIT = 56 * 1024 * 1024


def _cparams(sem):
    return pltpu.CompilerParams(dimension_semantics=sem, vmem_limit_bytes=VMEM_LIMIT)


def _rmsnorm(x, g):
    ms = jnp.mean(x * x, axis=-1, keepdims=True)
    return x * lax.rsqrt(ms + NORM_EPS) * g


def _dot(a, b):
    return jnp.dot(a, b, preferred_element_type=F32)


def _dot_t(a, b):
    return lax.dot_general(a, b, (((1,), (1,)), ((), ())), preferred_element_type=F32)


def _split_bf16(x):
    hi = x.astype(BF16)
    lo = (x - hi.astype(F32)).astype(BF16)
    return hi, lo


def _resident(shape):
    nd = len(shape)
    return pl.BlockSpec(shape, lambda *_: (0,) * nd, pipeline_mode=pl.Buffered(1))


def _norm_matmul_kernel(h_ref, g_ref, w_ref, o_ref, *, n_chunk):
    hn = _rmsnorm(h_ref[...], g_ref[...]).astype(BF16)
    for c in range(0, w_ref.shape[1], n_chunk):
        o_ref[:, c:c + n_chunk] = _dot(hn, w_ref[:, c:c + n_chunk]).astype(o_ref.dtype)


def _norm_matmul(h, g, w):
    t, d = h.shape
    n = w.shape[1]
    return pl.pallas_call(
        functools.partial(_norm_matmul_kernel, n_chunk=min(n, 1024)),
        out_shape=jax.ShapeDtypeStruct((t, n), BF16),
        grid=(t // ROW_TILE,),
        in_specs=[pl.BlockSpec((ROW_TILE, d), lambda i: (i, 0)),
                  _resident((1, d)),
                  _resident((d, n))],
        out_specs=pl.BlockSpec((ROW_TILE, n), lambda i: (i, 0)),
        compiler_params=_cparams(("parallel",)),
        name="norm_qkv",
    )(h, g.reshape(1, d), w)


def _stacked_idx(rows, width):
    rr = lax.broadcasted_iota(jnp.int32, (2 * rows, width), 0)
    cc = lax.broadcasted_iota(jnp.int32, (2 * rows, width), 1)
    return jnp.where(rr >= rows, rr - rows, rr), cc


def _sb_attn_kernel(q_ref, k_ref, v_ref, o_ref, acc_ref, *, seq, scale):
    n_pairs = q_ref.shape[1] // LANES
    nq = seq // SB_TQ
    tail = seq - nq * SB_TQ
    low_lanes = lax.broadcasted_iota(jnp.int32, (1, LANES), 1) < HEAD_DIM
    j_loc, s_loc = _stacked_idx(QB, QB)
    from_s = (j_loc >= s_loc).astype(BF16)

    def cols(pr):
        return slice(pr * LANES, (pr + 1) * LANES)

    def stack_heads(x):
        zero = jnp.zeros_like(x)
        return jnp.concatenate([jnp.where(low_lanes, x, zero), jnp.where(low_lanes, zero, x)], axis=0)

    def chunk(qzs, k0, cs, rows, mask):
        new_cs = []
        for g0 in range(0, n_pairs, SB_GROUP):
            group = range(g0, min(g0 + SB_GROUP, n_pairs))
            neg_zs, log_keeps = [], []
            for pr in group:
                nz = _dot_t(qzs[pr], k_ref[pl.ds(k0, TK), cols(pr)])
                soft = jnp.log(1.0 + jnp.exp2(jnp.abs(nz) * (-LOG2E)))
                log_keep = jnp.minimum(nz, 0.0) - soft
                if mask is not None:
                    log_keep = jnp.where(mask, log_keep, 0.0)
                neg_zs.append(nz)
                log_keeps.append(log_keep)
            halves = []
            for h0 in range(0, TK, QB):
                parts = []
                for lk in log_keeps:
                    hi, lo = _split_bf16(lk[:, h0:h0 + QB])
                    parts.append(jnp.concatenate([hi, lo], axis=1))
                halves.append(_dot(jnp.concatenate(parts, axis=0), from_s))
            for n, pr in enumerate(group):
                run = None if cs is None else cs[pr]
                ts = []
                for hh in reversed(range(TK // QB)):
                    keep_from = halves[hh][n * 2 * rows:(n + 1) * 2 * rows]
                    t = keep_from - neg_zs[n][:, hh * QB:(hh + 1) * QB]
                    if run is not None:
                        t = t + run
                    ts.append(t)
                    half_sum = keep_from[:, 0:1]
                    run = half_sum if run is None else run + half_sum
                p = jnp.exp2(jnp.concatenate(ts[::-1], axis=1) * LOG2E)
                if mask is not None:
                    p = jnp.where(mask, p, 0.0)
                pb = p.astype(BF16)
                contrib = _dot(jnp.concatenate([pb[:rows], pb[rows:]], axis=1),
                               stack_heads(v_ref[pl.ds(k0, TK), cols(pr)]))
                if cs is None:
                    acc_ref[0:rows, cols(pr)] = contrib
                else:
                    acc_ref[0:rows, cols(pr)] += contrib
                new_cs.append(run)
        return tuple(new_cs)

    def q_block(q0, rows, masked, n_below):
        qzs = []
        for pr in range(n_pairs):
            qs = (q_ref[pl.ds(q0, rows), cols(pr)].astype(F32) * (-scale)).astype(BF16)
            qzs.append(stack_heads(qs))
        cs = None
        for k0, mask in masked:
            cs = chunk(qzs, k0, cs, rows, mask)

        def below(idx, cs):
            return chunk(qzs, pl.multiple_of((n_below - 1 - idx) * TK, TK), cs, rows, None)

        lax.fori_loop(0, n_below, below, cs)
        o_ref[pl.ds(q0, rows), :] = acc_ref[0:rows, :].astype(o_ref.dtype)

    def q_loop(i, carry):
        q0 = pl.multiple_of(i * SB_TQ, SB_TQ)
        t_loc, s_loc = _stacked_idx(SB_TQ, TK)
        masked = []
        for m in reversed(range(SB_TQ // TK)):
            masked.append((q0 + m * TK, s_loc + m * TK < t_loc))
        q_block(q0, SB_TQ, masked, i * (SB_TQ // TK))
        return carry

    lax.fori_loop(0, nq, q_loop, 0)

    if tail:
        q0 = nq * SB_TQ
        k0 = seq - TK
        t_loc, s_loc = _stacked_idx(tail, TK)
        q_block(q0, tail, [(k0, (s_loc + k0 >= q0) & (s_loc + k0 < t_loc + q0))], q0 // TK)


def _attn_specs(seq, d_model):
    nblk = d_model // ATTN_LANES
    spec = lambda off: pl.BlockSpec((None, seq, ATTN_LANES), lambda b, c: (b, 0, off + c))
    return nblk, spec(0), spec(nblk), spec(2 * nblk)


def _sb_attention(qkv, batch, seq, d_model):
    qkv3 = qkv.reshape(batch, seq, 3 * d_model)
    nblk, q_spec, k_spec, v_spec = _attn_specs(seq, d_model)
    out = pl.pallas_call(
        functools.partial(_sb_attn_kernel, seq=seq, scale=HEAD_DIM ** -0.5),
        out_shape=jax.ShapeDtypeStruct((batch, seq, d_model), BF16),
        grid=(batch, nblk),
        in_specs=[q_spec, k_spec, v_spec],
        out_specs=q_spec,
        scratch_shapes=[pltpu.VMEM((SB_TQ, ATTN_LANES), F32)],
        compiler_params=_cparams(("parallel", "parallel")),
        name="sb_attention",
    )(qkv3, qkv3, qkv3)
    return out.reshape(batch * seq, d_model)


def _diff_attn_kernel(q_ref, k_ref, v_ref, cos_ref, sin_ref, lam_ref, subg_ref, o_ref, krot_ref, acc_ref,
                      *, seq, scale, lambda_init):
    n_heads = q_ref.shape[1] // LANES
    nq = seq // TQ
    tail = seq - nq * TQ
    lane = lax.broadcasted_iota(jnp.int32, (1, LANES), 1)
    low_lanes = lane < HEAD_DIM
    first_half = (lane & (HEAD_DIM // 2)) == 0

    lam_p = lam_ref[...]
    lam = (jnp.exp(jnp.sum(lam_p[0:1] * lam_p[1:2], axis=-1, keepdims=True))
           - jnp.exp(jnp.sum(lam_p[2:3] * lam_p[3:4], axis=-1, keepdims=True))
           + lambda_init)

    def cols(h):
        return slice(h * LANES, (h + 1) * LANES)

    def rope(x, r0, rows):
        partner = jnp.where(first_half, pltpu.roll(x, LANES - HEAD_DIM // 2, 1),
                            pltpu.roll(x, HEAD_DIM // 2, 1))
        return x * cos_ref[pl.ds(r0, rows), :] + partner * sin_ref[pl.ds(r0, rows), :]

    def rot_keys(r0, rows):
        for h in range(n_heads):
            kh = k_ref[pl.ds(r0, rows), cols(h)].astype(F32)
            krot_ref[pl.ds(r0, rows), cols(h)] = rope(kh, r0, rows).astype(BF16)

    def rot_loop(i, carry):
        rot_keys(pl.multiple_of(i * QB, QB), QB)
        return carry

    lax.fori_loop(0, seq // QB, rot_loop, 0)
    if seq % QB:
        rot_keys(seq - seq % QB, seq % QB)

    def stack_maps(x):
        zero = jnp.zeros_like(x)
        return jnp.concatenate([jnp.where(low_lanes, x, zero), jnp.where(low_lanes, zero, x)], axis=0)

    def chunk(qzs, k0, st, rows, mask):
        out = []
        for h in range(n_heads):
            s = _dot_t(qzs[h], krot_ref[pl.ds(k0, DIFF_TK), cols(h)])
            if mask is not None:
                s = jnp.where(mask, s, NEG_BIG)
            mx = jnp.max(s, axis=1, keepdims=True)
            vb = v_ref[pl.ds(k0, DIFF_TK), cols(h)]
            if st is None:
                p = jnp.exp(s - mx)
                acc_ref[h, 0:2 * rows, :] = _dot(p.astype(BF16), vb)
                out.extend([mx, jnp.sum(p, axis=1, keepdims=True)])
            else:
                m, l = st[2 * h], st[2 * h + 1]
                m_new = jnp.maximum(m, mx)
                alpha = jnp.exp(m - m_new)
                p = jnp.exp(s - m_new)
                acc_ref[h, 0:2 * rows, :] = alpha * acc_ref[h, 0:2 * rows, :] + _dot(p.astype(BF16), vb)
                out.extend([m_new, alpha * l + jnp.sum(p, axis=1, keepdims=True)])
        return tuple(out)

    def q_block(q0, rows, masked, n_below):
        qzs = []
        for h in range(n_heads):
            qh = rope(q_ref[pl.ds(q0, rows), cols(h)].astype(F32), q0, rows) * scale
            qzs.append(stack_maps(qh.astype(BF16)))
        st = None
        for k0, mask in masked:
            st = chunk(qzs, k0, st, rows, mask)

        def below(idx, st):
            return chunk(qzs, pl.multiple_of(idx * DIFF_TK, DIFF_TK), st, rows, None)

        st = lax.fori_loop(0, n_below, below, st)
        for h in range(n_heads):
            a = acc_ref[h, 0:2 * rows, :] / st[2 * h + 1]
            o = a[:rows] - lam * a[rows:]
            o = _rmsnorm(o, subg_ref[...]) * (1.0 - lambda_init)
            o_ref[pl.ds(q0, rows), cols(h)] = o.astype(o_ref.dtype)

    def q_loop(i, carry):
        q0 = pl.multiple_of(i * TQ, TQ)
        t_loc, s_loc = _stacked_idx(TQ, DIFF_TK)
        masked = [(q0 + m * DIFF_TK, s_loc + m * DIFF_TK <= t_loc) for m in range(TQ // DIFF_TK)]
        q_block(q0, TQ, masked, i * (TQ // DIFF_TK))
        return carry

    lax.fori_loop(0, nq, q_loop, 0)

    if tail:
        q0 = nq * TQ
        k0 = seq - DIFF_TK
        t_loc, s_loc = _stacked_idx(tail, DIFF_TK)
        q_block(q0, tail, [(k0, (s_loc + k0 >= q0) & (s_loc + k0 <= t_loc + q0))], q0 // DIFF_TK)


def _rope_tables(seq):
    inv = ROPE_THETA ** (-jnp.arange(0, HEAD_DIM, 2, dtype=F32) / HEAD_DIM)
    ang = jnp.arange(seq, dtype=F32)[:, None] * inv[None, :]
    cos, sin = jnp.cos(ang), jnp.sin(ang)
    reps = LANES // HEAD_DIM
    cos_t = jnp.tile(cos, (1, 2 * reps))
    sin_t = jnp.tile(jnp.concatenate([-sin, sin], axis=1), (1, reps))
    assert cos_t.shape == (seq, LANES) and sin_t.shape == (seq, LANES)
    return cos_t, sin_t


def _diff_attention(qkv, lam_params, subln_g, batch, seq, d_model, lambda_init):
    qkv3 = qkv.reshape(batch, seq, 3 * d_model)
    cos_t, sin_t = _rope_tables(seq)
    nblk, q_spec, k_spec, v_spec = _attn_specs(seq, d_model)
    out = pl.pallas_call(
        functools.partial(_diff_attn_kernel, seq=seq, scale=HEAD_DIM ** -0.5, lambda_init=lambda_init),
        out_shape=jax.ShapeDtypeStruct((batch, seq, d_model), BF16),
        grid=(batch, nblk),
        in_specs=[q_spec, k_spec, v_spec,
                  _resident((seq, LANES)), _resident((seq, LANES)),
                  _resident(lam_params.shape), _resident((1, LANES))],
        out_specs=q_spec,
        scratch_shapes=[pltpu.VMEM((seq, ATTN_LANES), BF16),
                        pltpu.VMEM((ATTN_LANES // LANES, 2 * TQ, LANES), F32)],
        compiler_params=_cparams(("parallel", "parallel")),
        name="diff_attention",
    )(qkv3, qkv3, qkv3, cos_t, sin_t, lam_params, subln_g.reshape(1, LANES))
    return out.reshape(batch * seq, d_model)


def _silu_mul(g, u):
    return g * (1.0 / (1.0 + jnp.exp(-g))) * u


def _wo_ffn_kernel(o_ref, h_ref, wo_ref, g_ref, wg_ref, wu_ref, wd_ref, out_ref, a_ref, *, f_chunk):
    h1 = h_ref[...] + _dot(o_ref[...], wo_ref[...])
    hn = _rmsnorm(h1, g_ref[...]).astype(BF16)
    for c in range(0, wg_ref.shape[1], f_chunk):
        gate = _dot(hn, wg_ref[:, c:c + f_chunk])
        up = _dot(hn, wu_ref[:, c:c + f_chunk])
        a_ref[:, c:c + f_chunk] = _silu_mul(gate, up).astype(BF16)
    out_ref[...] = h1 + _dot(a_ref[...], wd_ref[...])


def _wo_ffn(o, h, wo, g, wg, wu, wd):
    t, d = h.shape
    f = wg.shape[1]
    f_chunk = FF_CHUNK if f % FF_CHUNK == 0 else f
    row = lambda: pl.BlockSpec((ROW_TILE, d), lambda i: (i, 0))
    return pl.pallas_call(
        functools.partial(_wo_ffn_kernel, f_chunk=f_chunk),
        out_shape=jax.ShapeDtypeStruct((t, d), F32),
        grid=(t // ROW_TILE,),
        in_specs=[row(), row(), _resident((d, d)), _resident((1, d)),
                  _resident((d, f)), _resident((d, f)), _resident((f, d))],
        out_specs=row(),
        scratch_shapes=[pltpu.VMEM((ROW_TILE, f), BF16)],
        compiler_params=_cparams(("parallel",)),
        name="wo_ffn",
    )(o, h, wo, g.reshape(1, d), wg, wu, wd)


def _wo_router_kernel(o_ref, h_ref, wo_ref, g_ref, wr_hi_ref, wr_lo_ref,
                      h1_ref, hn_ref, gate_ref, meta_ref, cnt_ref, run_ref, *, n_exp):
    rows = h_ref.shape[0]

    @pl.when(pl.program_id(0) == 0)
    def _():
        run_ref[...] = jnp.zeros_like(run_ref)

    h1 = h_ref[...] + _dot(o_ref[...], wo_ref[...])
    h1_ref[...] = h1
    hn = _rmsnorm(h1, g_ref[...])
    hn_ref[...] = hn

    x_hi, x_lo = _split_bf16(hn)
    logits = _dot(x_hi, wr_hi_ref[...]) + (_dot(x_lo, wr_hi_ref[...]) + _dot(x_hi, wr_lo_ref[...]))
    lane = lax.broadcasted_iota(jnp.int32, (rows, LANES), 1)
    logits = jnp.where(lane < n_exp, logits, NEG_BIG)

    def top1(lg):
        m = jnp.max(lg, axis=1, keepdims=True)
        idx = jnp.min(jnp.where(lg == m, lane, LANES), axis=1, keepdims=True)
        return m, idx

    v1, i1 = top1(logits)
    v2, i2 = top1(jnp.where(lane == i1, NEG_BIG, logits))
    e = jnp.exp(v2 - v1)
    g1 = 1.0 / (1.0 + e)
    g2 = e / (1.0 + e)

    oh1 = (lane == i1).astype(F32)
    oh2 = (lane == i2).astype(F32)
    both = oh1 + oh2
    rr = lax.broadcasted_iota(jnp.int32, (rows, rows), 0)
    cc = lax.broadcasted_iota(jnp.int32, (rows, rows), 1)
    earlier = (cc < rr).astype(BF16)
    base = run_ref[...] + _dot(earlier, both.astype(BF16))
    rank1 = jnp.sum(oh1 * base, axis=1, keepdims=True)
    rank2 = jnp.sum(oh2 * base, axis=1, keepdims=True)
    run_ref[...] = run_ref[...] + jnp.sum(both, axis=0, keepdims=True)
    cnt_ref[...] = run_ref[...]

    gate_ref[...] = jnp.where(lane == 0, g1, jnp.where(lane == 1, g2, 0.0))
    packed = jnp.where(lane == 0, i1.astype(F32),
                       jnp.where(lane == 1, i2.astype(F32),
                                 jnp.where(lane == 2, rank1, jnp.where(lane == 3, rank2, 0.0))))
    meta_ref[...] = jnp.transpose(packed)[0:8, :].astype(jnp.int32)


def _wo_router(o, h, wo, g, w_router, n_exp):
    t, d = h.shape
    n_tiles = t // ROW_TILE
    wr = jnp.zeros((d, LANES), F32).at[:, :n_exp].set(w_router)
    wr_hi = wr.astype(BF16)
    wr_lo = (wr - wr_hi.astype(F32)).astype(BF16)
    row = lambda: pl.BlockSpec((ROW_TILE, d), lambda i: (i, 0))
    return pl.pallas_call(
        functools.partial(_wo_router_kernel, n_exp=n_exp),
        out_shape=(jax.ShapeDtypeStruct((t, d), F32),
                   jax.ShapeDtypeStruct((t, d), F32),
                   jax.ShapeDtypeStruct((t, LANES), F32),
                   jax.ShapeDtypeStruct((n_tiles, 8, ROW_TILE), jnp.int32),
                   jax.ShapeDtypeStruct((1, LANES), F32)),
        grid=(n_tiles,),
        in_specs=[row(), row(), _resident((d, d)), _resident((1, d)),
                  _resident((d, LANES)), _resident((d, LANES))],
        out_specs=(row(), row(),
                   pl.BlockSpec((ROW_TILE, LANES), lambda i: (i, 0)),
                   pl.BlockSpec((None, 8, ROW_TILE), lambda i: (i, 0, 0)),
                   pl.BlockSpec((1, LANES), lambda i: (0, 0))),
        scratch_shapes=[pltpu.VMEM((1, LANES), F32)],
        compiler_params=_cparams(("arbitrary",)),
        name="wo_router",
    )(o, h, wo, g.reshape(1, d), wr_hi, wr_lo)


def _row_copy(src, s, dst, d, sem):
    return pltpu.make_async_copy(src.at[pl.ds(s, 1)], dst.at[pl.ds(d, 1)], sem)


def _dispatch_kernel(starts_ref, counts_ref, meta_ref, hn_ref, xs_hbm, zero_ref, sem, *, n_exp):
    i = pl.program_id(0)
    rows = meta_ref.shape[1]
    zrows = zero_ref.shape[0]

    @pl.when(i == 0)
    def _():
        zero_ref[...] = jnp.zeros_like(zero_ref)
        used = starts_ref[n_exp]
        n_blocks = (xs_hbm.shape[0] - used) // zrows

        def tail_copy(r):
            dst = xs_hbm.at[pl.ds(pl.multiple_of(used + r * zrows, zrows), zrows)]
            return pltpu.make_async_copy(zero_ref, dst, sem)

        def tail_fill(r, c):
            tail_copy(r).start()
            return c

        def tail_drain(r, c):
            tail_copy(r).wait()
            return c

        lax.fori_loop(0, n_blocks, tail_fill, 0)
        lax.fori_loop(0, n_blocks, tail_drain, 0)

        for e in range(n_exp):
            first = starts_ref[e] + counts_ref[e]
            n_pad = starts_ref[e + 1] - first

            def fill(r, c):
                _row_copy(zero_ref, 0, xs_hbm, first + r, sem).start()
                return c

            def drain(r, c):
                _row_copy(zero_ref, 0, xs_hbm, first + r, sem).wait()
                return c

            lax.fori_loop(0, n_pad, fill, 0)
            lax.fori_loop(0, n_pad, drain, 0)

    def slots(r):
        return (starts_ref[meta_ref[0, r]] + meta_ref[2, r],
                starts_ref[meta_ref[1, r]] + meta_ref[3, r])

    def issue(r, c):
        p1, p2 = slots(r)
        _row_copy(hn_ref, r, xs_hbm, p1, sem).start()
        _row_copy(hn_ref, r, xs_hbm, p2, sem).start()
        return c

    def drain(r, c):
        p1, p2 = slots(r)
        _row_copy(hn_ref, r, xs_hbm, p1, sem).wait()
        _row_copy(hn_ref, r, xs_hbm, p2, sem).wait()
        return c

    lax.fori_loop(0, rows, issue, 0, unroll=DMA_UNROLL)
    lax.fori_loop(0, rows, drain, 0, unroll=DMA_UNROLL)


def _dispatch(starts, counts, meta, hn, n_sorted, n_exp):
    t, d = hn.shape
    n_tiles = meta.shape[0]
    return pl.pallas_call(
        functools.partial(_dispatch_kernel, n_exp=n_exp),
        out_shape=jax.ShapeDtypeStruct((n_sorted, d), F32),
        grid_spec=pltpu.PrefetchScalarGridSpec(
            num_scalar_prefetch=2,
            grid=(n_tiles,),
            in_specs=[pl.BlockSpec((None, 8, ROW_TILE), lambda i, *_: (i, 0, 0), memory_space=pltpu.SMEM),
                      pl.BlockSpec((ROW_TILE, d), lambda i, *_: (i, 0))],
            out_specs=pl.BlockSpec(memory_space=pl.ANY),
            scratch_shapes=[pltpu.VMEM((8, d), F32), pltpu.SemaphoreType.DMA(())]),
        compiler_params=_cparams(("arbitrary",)),
        name="moe_dispatch",
    )(starts, counts, meta, hn)


def _expert_ffn_kernel(te_ref, nact_ref, xs_ref, wg_ref, wu_ref, wd_ref, y_ref, xb_ref, a_ref, acc_ref):
    i = pl.program_id(0)
    j = pl.program_id(1)

    @pl.when(i < nact_ref[0])
    def _():
        @pl.when(j == 0)
        def _():
            xb_ref[...] = xs_ref[...].astype(BF16)
            acc_ref[...] = jnp.zeros_like(acc_ref)

        xb = xb_ref[...]
        for c in range(0, a_ref.shape[1], FF_CHUNK):
            gate = _dot(xb, wg_ref[:, c:c + FF_CHUNK])
            up = _dot(xb, wu_ref[:, c:c + FF_CHUNK])
            a_ref[:, c:c + FF_CHUNK] = _silu_mul(gate, up).astype(BF16)
        acc_ref[...] += _dot(a_ref[...], wd_ref[...])

        @pl.when(j == pl.num_programs(1) - 1)
        def _():
            y_ref[...] = acc_ref[...]

    @pl.when((i >= nact_ref[0]) & (j == pl.num_programs(1) - 1))
    def _():
        y_ref[...] = jnp.zeros_like(y_ref)


def _expert_ffn(tile_expert, n_active, xs, wg, wu, wd):
    n_sorted, d = xs.shape
    n_tiles = n_sorted // EXPERT_TILE
    f = wg.shape[2]
    fc = EXPERT_FF_CHUNK if f % EXPERT_FF_CHUNK == 0 else f
    nf = f // fc

    def tile(i, na):
        return jnp.minimum(i, na[0] - 1)

    def fchunk(i, j, na):
        return jnp.where(i < na[0], j, nf - 1)

    return pl.pallas_call(
        _expert_ffn_kernel,
        out_shape=jax.ShapeDtypeStruct((n_sorted, d), F32),
        grid_spec=pltpu.PrefetchScalarGridSpec(
            num_scalar_prefetch=2,
            grid=(n_tiles, nf),
            in_specs=[pl.BlockSpec((EXPERT_TILE, d), lambda i, j, te, na: (tile(i, na), 0)),
                      pl.BlockSpec((None, d, fc), lambda i, j, te, na: (te[i], 0, fchunk(i, j, na))),
                      pl.BlockSpec((None, d, fc), lambda i, j, te, na: (te[i], 0, fchunk(i, j, na))),
                      pl.BlockSpec((None, fc, d), lambda i, j, te, na: (te[i], fchunk(i, j, na), 0))],
            out_specs=pl.BlockSpec((EXPERT_TILE, d), lambda i, j, te, na: (i, 0)),
            scratch_shapes=[pltpu.VMEM((EXPERT_TILE, d), BF16), pltpu.VMEM((EXPERT_TILE, fc), BF16),
                            pltpu.VMEM((EXPERT_TILE, d), F32)]),
        compiler_params=_cparams(("arbitrary", "arbitrary")),
        name="expert_ffn",
    )(tile_expert, n_active, xs, wg, wu, wd)


def _combine_kernel(starts_ref, meta_ref, gate_ref, h1_ref, gfin_ref, y_hbm, out_ref, ya_ref, yb_ref, sem):
    rows = h1_ref.shape[0]

    def slots(r):
        return (starts_ref[meta_ref[0, r]] + meta_ref[2, r],
                starts_ref[meta_ref[1, r]] + meta_ref[3, r])

    def issue(r, c):
        p1, p2 = slots(r)
        _row_copy(y_hbm, p1, ya_ref, r, sem).start()
        _row_copy(y_hbm, p2, yb_ref, r, sem).start()
        return c

    def drain(r, c):
        p1, p2 = slots(r)
        _row_copy(y_hbm, p1, ya_ref, r, sem).wait()
        _row_copy(y_hbm, p2, yb_ref, r, sem).wait()
        return c

    lax.fori_loop(0, rows, issue, 0, unroll=DMA_UNROLL)
    lax.fori_loop(0, rows, drain, 0, unroll=DMA_UNROLL)
    gates = gate_ref[...]
    h2 = h1_ref[...] + (gates[:, 0:1] * ya_ref[...] + gates[:, 1:2] * yb_ref[...])
    out_ref[...] = _rmsnorm(h2, gfin_ref[...])


def _combine(starts, meta, gates, h1, gfin, y):
    t, d = h1.shape
    n_tiles = meta.shape[0]
    return pl.pallas_call(
        _combine_kernel,
        out_shape=jax.ShapeDtypeStruct((t, d), F32),
        grid_spec=pltpu.PrefetchScalarGridSpec(
            num_scalar_prefetch=1,
            grid=(n_tiles,),
            in_specs=[pl.BlockSpec((None, 8, ROW_TILE), lambda i, *_: (i, 0, 0), memory_space=pltpu.SMEM),
                      pl.BlockSpec((ROW_TILE, LANES), lambda i, *_: (i, 0)),
                      pl.BlockSpec((ROW_TILE, d), lambda i, *_: (i, 0)),
                      pl.BlockSpec((1, d), lambda i, *_: (0, 0)),
                      pl.BlockSpec(memory_space=pl.ANY)],
            out_specs=pl.BlockSpec((ROW_TILE, d), lambda i, *_: (i, 0)),
            scratch_shapes=[pltpu.VMEM((ROW_TILE, d), F32), pltpu.VMEM((ROW_TILE, d), F32),
                            pltpu.SemaphoreType.DMA(())]),
        compiler_params=_cparams(("arbitrary",)),
        name="moe_combine",
    )(starts, meta, gates, h1, gfin.reshape(1, d), y)


def kernel(x, meta_tokens, norm_gains, sb_w_qkv, sb_w_o, diff_w_qkv, diff_w_o, diff_subln,
           diff_lambda_q1, diff_lambda_k1, diff_lambda_q2, diff_lambda_k2,
           ffn_w_gate, ffn_w_up, ffn_w_down, moe_router, moe_w_gate, moe_w_up, moe_w_down,
           final_norm):
    batch, seq_in, d = x.shape
    n_meta = meta_tokens.shape[0]
    seq = n_meta + seq_in
    t = batch * seq
    n_exp = moe_router.shape[-1]
    assert d % LANES == 0 and t % ROW_TILE == 0 and seq % 16 == 0 and seq >= TQ and seq % TQ < TK
    assert diff_subln.shape[-1] == LANES and n_exp <= 8

    meta = jnp.broadcast_to(meta_tokens.astype(x.dtype)[None], (batch, n_meta, d))
    h = jnp.concatenate([meta, x], axis=1).reshape(t, d)

    qkv = _norm_matmul(h, norm_gains[0, 0], sb_w_qkv[0].astype(BF16))
    o = _sb_attention(qkv, batch, seq, d)
    h = _wo_ffn(o, h, sb_w_o[0].astype(BF16), norm_gains[0, 1],
                ffn_w_gate[0].astype(BF16), ffn_w_up[0].astype(BF16), ffn_w_down[0].astype(BF16))

    lambda_init = 0.8 - 0.6 * math.exp(-0.3 * 1)
    qkv = _norm_matmul(h, norm_gains[1, 0], diff_w_qkv[0].astype(BF16))
    lam_params = jnp.stack([diff_lambda_q1[0], diff_lambda_k1[0], diff_lambda_q2[0], diff_lambda_k2[0]])
    o = _diff_attention(qkv, lam_params, diff_subln[0], batch, seq, d, lambda_init)
    h1, hn, gates, route, counts = _wo_router(o, h, diff_w_o[0].astype(BF16), norm_gains[1, 1],
                                              moe_router[0], n_exp)

    counts = counts[0, :n_exp].astype(jnp.int32)
    tiles_per = (counts + EXPERT_TILE - 1) // EXPERT_TILE
    tile_end = jnp.cumsum(tiles_per)
    starts = jnp.concatenate([jnp.zeros((1,), jnp.int32), tile_end * EXPERT_TILE]).astype(jnp.int32)
    n_tiles = (TOP_K_SLOTS * t + EXPERT_TILE - 1) // EXPERT_TILE + n_exp
    n_active = tile_end[-1:].astype(jnp.int32)
    tile_ids = jnp.minimum(jnp.arange(n_tiles, dtype=jnp.int32), n_active[0] - 1)
    tile_expert = jnp.sum((tile_ids[:, None] >= tile_end[None, :]).astype(jnp.int32), axis=1)

    xs = _dispatch(starts, counts, route, hn, n_tiles * EXPERT_TILE, n_exp)
    y = _expert_ffn(tile_expert, n_active, xs, moe_w_gate[0].astype(BF16), moe_w_up[0].astype(BF16),
                    moe_w_down[0].astype(BF16))
    out = _combine(starts, route, gates, h1, final_norm, y)
    return out.reshape(batch, seq, d)[:, n_meta:]
```

```python
import functools
import math

import jax
import jax.numpy as jnp
from jax import lax
from jax.experimental import pallas as pl
from jax.experimental.pallas import tpu as pltpu

F32 = jnp.float32
BF16 = jnp.bfloat16

NORM_EPS = 1e-6
ROPE_THETA = 10000.0
HEAD_DIM = 64
LANES = 128
SUBLANES = 8
ZERO_ROWS = 8
QB = 128
TQ = 512
SB_TQ = 512
TK = 256
DIFF_TK = 512
SB_GROUP = 2
ATTN_LANES = 512
ROW_TILE = 512
EXPERT_TILE = 1024
EXPERT_FF_CHUNK = 1792
FF_CHUNK = 256
DMA_UNROLL = 8
TOP_K_SLOTS = 2
NEG_BIG = -1e30
LOG2E = 1.4426950408889634
VMEM_LIMIT = 56 * 1024 * 1024


def _cparams(sem):
    return pltpu.CompilerParams(dimension_semantics=sem, vmem_limit_bytes=VMEM_LIMIT)


def _rmsnorm(x, g):
    ms = jnp.mean(x * x, axis=-1, keepdims=True)
    return x * lax.rsqrt(ms + NORM_EPS) * g


def _dot(a, b):
    return jnp.dot(a, b, preferred_element_type=F32)


def _dot_t(a, b):
    return lax.dot_general(a, b, (((1,), (1,)), ((), ())), preferred_element_type=F32)


def _split_bf16(x):
    hi = x.astype(BF16)
    lo = (x - hi.astype(F32)).astype(BF16)
    return hi, lo


def _resident(shape):
    nd = len(shape)
    return pl.BlockSpec(shape, lambda *_: (0,) * nd, pipeline_mode=pl.Buffered(1))


def _norm_matmul_kernel(h_ref, g_ref, w_ref, o_ref, *, n_chunk):
    hn = _rmsnorm(h_ref[...], g_ref[...]).astype(BF16)
    for c in range(0, w_ref.shape[1], n_chunk):
        o_ref[:, c:c + n_chunk] = _dot(hn, w_ref[:, c:c + n_chunk]).astype(o_ref.dtype)


def _norm_matmul(h, g, w):
    t, d = h.shape
    n = w.shape[1]
    return pl.pallas_call(
        functools.partial(_norm_matmul_kernel, n_chunk=min(n, 1024)),
        out_shape=jax.ShapeDtypeStruct((t, n), BF16),
        grid=(t // ROW_TILE,),
        in_specs=[pl.BlockSpec((ROW_TILE, d), lambda i: (i, 0)),
                  _resident((1, d)),
                  _resident((d, n))],
        out_specs=pl.BlockSpec((ROW_TILE, n), lambda i: (i, 0)),
        compiler_params=_cparams(("parallel",)),
        name="norm_qkv",
    )(h, g.reshape(1, d), w)


def _stacked_idx(rows, width):
    rr = lax.broadcasted_iota(jnp.int32, (2 * rows, width), 0)
    cc = lax.broadcasted_iota(jnp.int32, (2 * rows, width), 1)
    return jnp.where(rr >= rows, rr - rows, rr), cc


def _sb_attn_kernel(q_ref, k_ref, v_ref, o_ref, acc_ref, *, seq, scale):
    n_pairs = q_ref.shape[1] // LANES
    nq = seq // SB_TQ
    tail = seq - nq * SB_TQ
    low_lanes = lax.broadcasted_iota(jnp.int32, (1, LANES), 1) < HEAD_DIM
    j_loc, s_loc = _stacked_idx(QB, QB)
    from_s = (j_loc >= s_loc).astype(BF16)

    def cols(pr):
        return slice(pr * LANES, (pr + 1) * LANES)

    def stack_heads(x):
        zero = jnp.zeros_like(x)
        return jnp.concatenate([jnp.where(low_lanes, x, zero), jnp.where(low_lanes, zero, x)], axis=0)

    def chunk(qzs, k0, cs, rows, mask):
        new_cs = []
        for g0 in range(0, n_pairs, SB_GROUP):
            group = range(g0, min(g0 + SB_GROUP, n_pairs))
            neg_zs, log_keeps = [], []
            for pr in group:
                nz = _dot_t(qzs[pr], k_ref[pl.ds(k0, TK), cols(pr)])
                soft = jnp.log(1.0 + jnp.exp2(jnp.abs(nz) * (-LOG2E)))
                log_keep = jnp.minimum(nz, 0.0) - soft
                if mask is not None:
                    log_keep = jnp.where(mask, log_keep, 0.0)
                neg_zs.append(nz)
                log_keeps.append(log_keep)
            halves = []
            for h0 in range(0, TK, QB):
                parts = []
                for lk in log_keeps:
                    hi, lo = _split_bf16(lk[:, h0:h0 + QB])
                    parts.append(jnp.concatenate([hi, lo], axis=1))
                halves.append(_dot(jnp.concatenate(parts, axis=0), from_s))
            for n, pr in enumerate(group):
                run = None if cs is None else cs[pr]
                ts = []
                for hh in reversed(range(TK // QB)):
                    keep_from = halves[hh][n * 2 * rows:(n + 1) * 2 * rows]
                    t = keep_from - neg_zs[n][:, hh * QB:(hh + 1) * QB]
                    if run is not None:
                        t = t + run
                    ts.append(t)
                    half_sum = keep_from[:, 0:1]
                    run = half_sum if run is None else run + half_sum
                p = jnp.exp2(jnp.concatenate(ts[::-1], axis=1) * LOG2E)
                if mask is not None:
                    p = jnp.where(mask, p, 0.0)
                pb = p.astype(BF16)
                contrib = _dot(jnp.concatenate([pb[:rows], pb[rows:]], axis=1),
                               stack_heads(v_ref[pl.ds(k0, TK), cols(pr)]))
                if cs is None:
                    acc_ref[0:rows, cols(pr)] = contrib
                else:
                    acc_ref[0:rows, cols(pr)] += contrib
                new_cs.append(run)
        return tuple(new_cs)

    def q_block(q0, rows, masked, n_below):
        qzs = []
        for pr in range(n_pairs):
            qs = (q_ref[pl.ds(q0, rows), cols(pr)].astype(F32) * (-scale)).astype(BF16)
            qzs.append(stack_heads(qs))
        cs = None
        for k0, mask in masked:
            cs = chunk(qzs, k0, cs, rows, mask)

        def below(idx, cs):
            return chunk(qzs, pl.multiple_of((n_below - 1 - idx) * TK, TK), cs, rows, None)

        lax.fori_loop(0, n_below, below, cs)
        o_ref[pl.ds(q0, rows), :] = acc_ref[0:rows, :].astype(o_ref.dtype)

    def q_loop(i, carry):
        q0 = pl.multiple_of(i * SB_TQ, SB_TQ)
        t_loc, s_loc = _stacked_idx(SB_TQ, TK)
        masked = []
        for m in reversed(range(SB_TQ // TK)):
            masked.append((q0 + m * TK, s_loc + m * TK < t_loc))
        q_block(q0, SB_TQ, masked, i * (SB_TQ // TK))
        return carry

    lax.fori_loop(0, nq, q_loop, 0)

    if tail:
        q0 = nq * SB_TQ
        k0 = seq - TK
        t_loc, s_loc = _stacked_idx(tail, TK)
        q_block(q0, tail, [(k0, (s_loc + k0 >= q0) & (s_loc + k0 < t_loc + q0))], q0 // TK)


def _attn_specs(seq, d_model):
    nblk = d_model // ATTN_LANES
    spec = lambda off: pl.BlockSpec((None, seq, ATTN_LANES), lambda b, c: (b, 0, off + c))
    return nblk, spec(0), spec(nblk), spec(2 * nblk)


def _sb_attention(qkv, batch, seq, d_model):
    qkv3 = qkv.reshape(batch, seq, 3 * d_model)
    nblk, q_spec, k_spec, v_spec = _attn_specs(seq, d_model)
    out = pl.pallas_call(
        functools.partial(_sb_attn_kernel, seq=seq, scale=HEAD_DIM ** -0.5),
        out_shape=jax.ShapeDtypeStruct((batch, seq, d_model), BF16),
        grid=(batch, nblk),
        in_specs=[q_spec, k_spec, v_spec],
        out_specs=q_spec,
        scratch_shapes=[pltpu.VMEM((SB_TQ, ATTN_LANES), F32)],
        compiler_params=_cparams(("parallel", "parallel")),
        name="sb_attention",
    )(qkv3, qkv3, qkv3)
    return out.reshape(batch * seq, d_model)


def _diff_attn_kernel(q_ref, k_ref, v_ref, cos_ref, sin_ref, lam_ref, subg_ref, o_ref, krot_ref, acc_ref,
                      *, seq, scale, lambda_init):
    n_heads = q_ref.shape[1] // LANES
    nq = seq // TQ
    tail = seq - nq * TQ
    lane = lax.broadcasted_iota(jnp.int32, (1, LANES), 1)
    low_lanes = lane < HEAD_DIM
    first_half = (lane & (HEAD_DIM // 2)) == 0

    lam_p = lam_ref[...]
    lam = (jnp.exp(jnp.sum(lam_p[0:1] * lam_p[1:2], axis=-1, keepdims=True))
           - jnp.exp(jnp.sum(lam_p[2:3] * lam_p[3:4], axis=-1, keepdims=True))
           + lambda_init)

    def cols(h):
        return slice(h * LANES, (h + 1) * LANES)

    def rope(x, r0, rows):
        partner = jnp.where(first_half, pltpu.roll(x, LANES - HEAD_DIM // 2, 1),
                            pltpu.roll(x, HEAD_DIM // 2, 1))
        return x * cos_ref[pl.ds(r0, rows), :] + partner * sin_ref[pl.ds(r0, rows), :]

    def rot_keys(r0, rows):
        for h in range(n_heads):
            kh = k_ref[pl.ds(r0, rows), cols(h)].astype(F32)
            krot_ref[pl.ds(r0, rows), cols(h)] = rope(kh, r0, rows).astype(BF16)

    def rot_loop(i, carry):
        rot_keys(pl.multiple_of(i * QB, QB), QB)
        return carry

    lax.fori_loop(0, seq // QB, rot_loop, 0)
    if seq % QB:
        rot_keys(seq - seq % QB, seq % QB)

    def stack_maps(x):
        zero = jnp.zeros_like(x)
        return jnp.concatenate([jnp.where(low_lanes, x, zero), jnp.where(low_lanes, zero, x)], axis=0)

    def chunk(qzs, k0, st, rows, mask):
        out = []
        for h in range(n_heads):
            s = _dot_t(qzs[h], krot_ref[pl.ds(k0, DIFF_TK), cols(h)])
            if mask is not None:
                s = jnp.where(mask, s, NEG_BIG)
            mx = jnp.max(s, axis=1, keepdims=True)
            vb = v_ref[pl.ds(k0, DIFF_TK), cols(h)]
            if st is None:
                p = jnp.exp(s - mx)
                acc_ref[h, 0:2 * rows, :] = _dot(p.astype(BF16), vb)
                out.extend([mx, jnp.sum(p, axis=1, keepdims=True)])
            else:
                m, l = st[2 * h], st[2 * h + 1]
                m_new = jnp.maximum(m, mx)
                alpha = jnp.exp(m - m_new)
                p = jnp.exp(s - m_new)
                acc_ref[h, 0:2 * rows, :] = alpha * acc_ref[h, 0:2 * rows, :] + _dot(p.astype(BF16), vb)
                out.extend([m_new, alpha * l + jnp.sum(p, axis=1, keepdims=True)])
        return tuple(out)

    def q_block(q0, rows, masked, n_below):
        qzs = []
        for h in range(n_heads):
            qh = rope(q_ref[pl.ds(q0, rows), cols(h)].astype(F32), q0, rows) * scale
            qzs.append(stack_maps(qh.astype(BF16)))
        st = None
        for k0, mask in masked:
            st = chunk(qzs, k0, st, rows, mask)

        def below(idx, st):
            return chunk(qzs, pl.multiple_of(idx * DIFF_TK, DIFF_TK), st, rows, None)

        st = lax.fori_loop(0, n_below, below, st)
        for h in range(n_heads):
            a = acc_ref[h, 0:2 * rows, :] / st[2 * h + 1]
            o = a[:rows] - lam * a[rows:]
            o = _rmsnorm(o, subg_ref[...]) * (1.0 - lambda_init)
            o_ref[pl.ds(q0, rows), cols(h)] = o.astype(o_ref.dtype)

    def q_loop(i, carry):
        q0 = pl.multiple_of(i * TQ, TQ)
        t_loc, s_loc = _stacked_idx(TQ, DIFF_TK)
        masked = [(q0 + m * DIFF_TK, s_loc + m * DIFF_TK <= t_loc) for m in range(TQ // DIFF_TK)]
        q_block(q0, TQ, masked, i * (TQ // DIFF_TK))
        return carry

    lax.fori_loop(0, nq, q_loop, 0)

    if tail:
        q0 = nq * TQ
        k0 = seq - DIFF_TK
        t_loc, s_loc = _stacked_idx(tail, DIFF_TK)
        q_block(q0, tail, [(k0, (s_loc + k0 >= q0) & (s_loc + k0 <= t_loc + q0))], q0 // DIFF_TK)


def _rope_tables(seq):
    inv = ROPE_THETA ** (-jnp.arange(0, HEAD_DIM, 2, dtype=F32) / HEAD_DIM)
    ang = jnp.arange(seq, dtype=F32)[:, None] * inv[None, :]
    cos, sin = jnp.cos(ang), jnp.sin(ang)
    reps = LANES // HEAD_DIM
    cos_t = jnp.tile(cos, (1, 2 * reps))
    sin_t = jnp.tile(jnp.concatenate([-sin, sin], axis=1), (1, reps))
    assert cos_t.shape == (seq, LANES) and sin_t.shape == (seq, LANES)
    return cos_t, sin_t


def _diff_attention(qkv, lam_params, subln_g, batch, seq, d_model, lambda_init):
    qkv3 = qkv.reshape(batch, seq, 3 * d_model)
    cos_t, sin_t = _rope_tables(seq)
    nblk, q_spec, k_spec, v_spec = _attn_specs(seq, d_model)
    out = pl.pallas_call(
        functools.partial(_diff_attn_kernel, seq=seq, scale=HEAD_DIM ** -0.5, lambda_init=lambda_init),
        out_shape=jax.ShapeDtypeStruct((batch, seq, d_model), BF16),
        grid=(batch, nblk),
        in_specs=[q_spec, k_spec, v_spec,
                  _resident((seq, LANES)), _resident((seq, LANES)),
                  _resident(lam_params.shape), _resident((1, LANES))],
        out_specs=q_spec,
        scratch_shapes=[pltpu.VMEM((seq, ATTN_LANES), BF16),
                        pltpu.VMEM((ATTN_LANES // LANES, 2 * TQ, LANES), F32)],
        compiler_params=_cparams(("parallel", "parallel")),
        name="diff_attention",
    )(qkv3, qkv3, qkv3, cos_t, sin_t, lam_params, subln_g.reshape(1, LANES))
    return out.reshape(batch * seq, d_model)


def _silu_mul(g, u):
    return g * (1.0 / (1.0 + jnp.exp(-g))) * u


def _wo_ffn_kernel(o_ref, h_ref, wo_ref, g_ref, wg_ref, wu_ref, wd_ref, out_ref, a_ref, *, f_chunk):
    h1 = h_ref[...] + _dot(o_ref[...], wo_ref[...])
    hn = _rmsnorm(h1, g_ref[...]).astype(BF16)
    for c in range(0, wg_ref.shape[1], f_chunk):
        gate = _dot(hn, wg_ref[:, c:c + f_chunk])
        up = _dot(hn, wu_ref[:, c:c + f_chunk])
        a_ref[:, c:c + f_chunk] = _silu_mul(gate, up).astype(BF16)
    out_ref[...] = h1 + _dot(a_ref[...], wd_ref[...])


def _wo_ffn(o, h, wo, g, wg, wu, wd):
    t, d = h.shape
    f = wg.shape[1]
    f_chunk = FF_CHUNK if f % FF_CHUNK == 0 else f
    row = lambda: pl.BlockSpec((ROW_TILE, d), lambda i: (i, 0))
    return pl.pallas_call(
        functools.partial(_wo_ffn_kernel, f_chunk=f_chunk),
        out_shape=jax.ShapeDtypeStruct((t, d), F32),
        grid=(t // ROW_TILE,),
        in_specs=[row(), row(), _resident((d, d)), _resident((1, d)),
                  _resident((d, f)), _resident((d, f)), _resident((f, d))],
        out_specs=row(),
        scratch_shapes=[pltpu.VMEM((ROW_TILE, f), BF16)],
        compiler_params=_cparams(("parallel",)),
        name="wo_ffn",
    )(o, h, wo, g.reshape(1, d), wg, wu, wd)


def _wo_router_kernel(o_ref, h_ref, wo_ref, g_ref, wr_hi_ref, wr_lo_ref,
                      h1_ref, hn_ref, gate_ref, meta_ref, cnt_ref, run_ref, *, n_exp):
    rows = h_ref.shape[0]

    @pl.when(pl.program_id(0) == 0)
    def _():
        run_ref[...] = jnp.zeros_like(run_ref)

    h1 = h_ref[...] + _dot(o_ref[...], wo_ref[...])
    h1_ref[...] = h1
    hn = _rmsnorm(h1, g_ref[...])
    hn_ref[...] = hn

    x_hi, x_lo = _split_bf16(hn)
    logits = _dot(x_hi, wr_hi_ref[...]) + (_dot(x_lo, wr_hi_ref[...]) + _dot(x_hi, wr_lo_ref[...]))
    lane = lax.broadcasted_iota(jnp.int32, (rows, LANES), 1)
    logits = jnp.where(lane < n_exp, logits, NEG_BIG)

    def top1(lg):
        m = jnp.max(lg, axis=1, keepdims=True)
        idx = jnp.min(jnp.where(lg == m, lane, LANES), axis=1, keepdims=True)
        return m, idx

    v1, i1 = top1(logits)
    v2, i2 = top1(jnp.where(lane == i1, NEG_BIG, logits))
    e = jnp.exp(v2 - v1)
    g1 = 1.0 / (1.0 + e)
    g2 = e / (1.0 + e)

    oh1 = (lane == i1).astype(F32)
    oh2 = (lane == i2).astype(F32)
    both = oh1 + oh2
    rr = lax.broadcasted_iota(jnp.int32, (rows, rows), 0)
    cc = lax.broadcasted_iota(jnp.int32, (rows, rows), 1)
    earlier = (cc < rr).astype(BF16)
    base = run_ref[...] + _dot(earlier, both.astype(BF16))
    rank1 = jnp.sum(oh1 * base, axis=1, keepdims=True)
    rank2 = jnp.sum(oh2 * base, axis=1, keepdims=True)
    run_ref[...] = run_ref[...] + jnp.sum(both, axis=0, keepdims=True)
    cnt_ref[...] = run_ref[...]

    gate_ref[...] = jnp.where(lane == 0, g1, jnp.where(lane == 1, g2, 0.0))
    packed = jnp.where(lane == 0, i1.astype(F32),
                       jnp.where(lane == 1, i2.astype(F32),
                                 jnp.where(lane == 2, rank1, jnp.where(lane == 3, rank2, 0.0))))
    meta_ref[...] = jnp.transpose(packed)[0:8, :].astype(jnp.int32)


def _wo_router(o, h, wo, g, w_router, n_exp):
    t, d = h.shape
    n_tiles = t // ROW_TILE
    wr = jnp.zeros((d, LANES), F32).at[:, :n_exp].set(w_router)
    wr_hi = wr.astype(BF16)
    wr_lo = (wr - wr_hi.astype(F32)).astype(BF16)
    row = lambda: pl.BlockSpec((ROW_TILE, d), lambda i: (i, 0))
    return pl.pallas_call(
        functools.partial(_wo_router_kernel, n_exp=n_exp),
        out_shape=(jax.ShapeDtypeStruct((t, d), F32),
                   jax.ShapeDtypeStruct((t, d), F32),
                   jax.ShapeDtypeStruct((t, LANES), F32),
                   jax.ShapeDtypeStruct((n_tiles, 8, ROW_TILE), jnp.int32),
                   jax.ShapeDtypeStruct((1, LANES), F32)),
        grid=(n_tiles,),
        in_specs=[row(), row(), _resident((d, d)), _resident((1, d)),
                  _resident((d, LANES)), _resident((d, LANES))],
        out_specs=(row(), row(),
                   pl.BlockSpec((ROW_TILE, LANES), lambda i: (i, 0)),
                   pl.BlockSpec((None, 8, ROW_TILE), lambda i: (i, 0, 0)),
                   pl.BlockSpec((1, LANES), lambda i: (0, 0))),
        scratch_shapes=[pltpu.VMEM((1, LANES), F32)],
        compiler_params=_cparams(("arbitrary",)),
        name="wo_router",
    )(o, h, wo, g.reshape(1, d), wr_hi, wr_lo)


def _row_copy(src, s, dst, d, sem):
    return pltpu.make_async_copy(src.at[pl.ds(pl.multiple_of(s * SUBLANES, SUBLANES), SUBLANES)],
                                 dst.at[pl.ds(pl.multiple_of(d * SUBLANES, SUBLANES), SUBLANES)], sem)


def _to_row_tiles(dst_ref, x):
    n = x.shape[0]
    for c in range(SUBLANES):
        dst_ref[pl.ds(c, n, stride=SUBLANES), :] = x[:, c * LANES:(c + 1) * LANES]


def _from_row_tiles(src_ref):
    n = src_ref.shape[0] // SUBLANES
    return jnp.concatenate([src_ref[pl.ds(c, n, stride=SUBLANES), :] for c in range(SUBLANES)], axis=1)


def _dispatch_kernel(starts_ref, counts_ref, meta_ref, hn_ref, xs_hbm, stage_ref, zero_ref, sem, *, n_exp):
    i = pl.program_id(0)
    rows = meta_ref.shape[1]
    zrows = zero_ref.shape[0] // SUBLANES
    n_sorted = xs_hbm.shape[0] // SUBLANES

    @pl.when(i == 0)
    def _():
        zero_ref[...] = jnp.zeros_like(zero_ref)
        used = starts_ref[n_exp]
        n_blocks = (n_sorted - used) // zrows

        def tail_copy(r):
            first_row = pl.multiple_of((used + r * zrows) * SUBLANES, zrows * SUBLANES)
            return pltpu.make_async_copy(zero_ref, xs_hbm.at[pl.ds(first_row, zrows * SUBLANES)], sem)

        def tail_fill(r, c):
            tail_copy(r).start()
            return c

        def tail_drain(r, c):
            tail_copy(r).wait()
            return c

        lax.fori_loop(0, n_blocks, tail_fill, 0)
        lax.fori_loop(0, n_blocks, tail_drain, 0)

        for e in range(n_exp):
            first = starts_ref[e] + counts_ref[e]
            n_pad = starts_ref[e + 1] - first

            def fill(r, c):
                _row_copy(zero_ref, 0, xs_hbm, first + r, sem).start()
                return c

            def drain(r, c):
                _row_copy(zero_ref, 0, xs_hbm, first + r, sem).wait()
                return c

            lax.fori_loop(0, n_pad, fill, 0)
            lax.fori_loop(0, n_pad, drain, 0)

    def slots(r):
        return (starts_ref[meta_ref[0, r]] + meta_ref[2, r],
                starts_ref[meta_ref[1, r]] + meta_ref[3, r])

    _to_row_tiles(stage_ref, hn_ref[...])

    def issue(r, c):
        p1, p2 = slots(r)
        _row_copy(stage_ref, r, xs_hbm, p1, sem).start(priority=0)
        _row_copy(stage_ref, r, xs_hbm, p2, sem).start(priority=1)
        return c

    def drain(r, c):
        p1, p2 = slots(r)
        _row_copy(stage_ref, r, xs_hbm, p1, sem).wait()
        _row_copy(stage_ref, r, xs_hbm, p2, sem).wait()
        return c

    lax.fori_loop(0, rows, issue, 0, unroll=DMA_UNROLL)
    lax.fori_loop(0, rows, drain, 0, unroll=DMA_UNROLL)


def _dispatch(starts, counts, meta, hn, n_sorted, n_exp):
    t, d = hn.shape
    assert d == SUBLANES * LANES
    n_tiles = meta.shape[0]
    return pl.pallas_call(
        functools.partial(_dispatch_kernel, n_exp=n_exp),
        out_shape=jax.ShapeDtypeStruct((n_sorted * SUBLANES, LANES), F32),
        grid_spec=pltpu.PrefetchScalarGridSpec(
            num_scalar_prefetch=2,
            grid=(n_tiles,),
            in_specs=[pl.BlockSpec((None, 8, ROW_TILE), lambda i, *_: (i, 0, 0), memory_space=pltpu.SMEM),
                      pl.BlockSpec((ROW_TILE, d), lambda i, *_: (i, 0))],
            out_specs=pl.BlockSpec(memory_space=pl.ANY),
            scratch_shapes=[pltpu.VMEM((ROW_TILE * SUBLANES, LANES), F32),
                            pltpu.VMEM((ZERO_ROWS * SUBLANES, LANES), F32),
                            pltpu.SemaphoreType.DMA(())]),
        compiler_params=_cparams(("arbitrary",)),
        name="moe_dispatch",
    )(starts, counts, meta, hn)


def _expert_ffn_kernel(te_ref, nact_ref, xs_ref, wg_ref, wu_ref, wd_ref, y_ref, xb_ref, a_ref, acc_ref):
    i = pl.program_id(0)
    j = pl.program_id(1)

    @pl.when(i < nact_ref[0])
    def _():
        @pl.when(j == 0)
        def _():
            xb_ref[...] = _from_row_tiles(xs_ref).astype(BF16)
            acc_ref[...] = jnp.zeros_like(acc_ref)

        xb = xb_ref[...]
        for c in range(0, a_ref.shape[1], FF_CHUNK):
            gate = _dot(xb, wg_ref[:, c:c + FF_CHUNK])
            up = _dot(xb, wu_ref[:, c:c + FF_CHUNK])
            a_ref[:, c:c + FF_CHUNK] = _silu_mul(gate, up).astype(BF16)
        acc_ref[...] += _dot(a_ref[...], wd_ref[...])

        @pl.when(j == pl.num_programs(1) - 1)
        def _():
            _to_row_tiles(y_ref, acc_ref[...])

    @pl.when((i >= nact_ref[0]) & (j == pl.num_programs(1) - 1))
    def _():
        y_ref[...] = jnp.zeros_like(y_ref)


def _expert_ffn(tile_expert, n_active, xs, wg, wu, wd):
    d = SUBLANES * LANES
    n_tiles = xs.shape[0] // (EXPERT_TILE * SUBLANES)
    f = wg.shape[2]
    fc = EXPERT_FF_CHUNK if f % EXPERT_FF_CHUNK == 0 else f
    nf = f // fc

    def tile(i, na):
        return jnp.minimum(i, na[0] - 1)

    def fchunk(i, j, na):
        return jnp.where(i < na[0], j, nf - 1)

    return pl.pallas_call(
        _expert_ffn_kernel,
        out_shape=jax.ShapeDtypeStruct(xs.shape, F32),
        grid_spec=pltpu.PrefetchScalarGridSpec(
            num_scalar_prefetch=2,
            grid=(n_tiles, nf),
            in_specs=[pl.BlockSpec((EXPERT_TILE * SUBLANES, LANES), lambda i, j, te, na: (tile(i, na), 0)),
                      pl.BlockSpec((None, d, fc), lambda i, j, te, na: (te[i], 0, fchunk(i, j, na))),
                      pl.BlockSpec((None, d, fc), lambda i, j, te, na: (te[i], 0, fchunk(i, j, na))),
                      pl.BlockSpec((None, fc, d), lambda i, j, te, na: (te[i], fchunk(i, j, na), 0))],
            out_specs=pl.BlockSpec((EXPERT_TILE * SUBLANES, LANES), lambda i, j, te, na: (i, 0)),
            scratch_shapes=[pltpu.VMEM((EXPERT_TILE, d), BF16), pltpu.VMEM((EXPERT_TILE, fc), BF16),
                            pltpu.VMEM((EXPERT_TILE, d), F32)]),
        compiler_params=_cparams(("arbitrary", "arbitrary")),
        name="expert_ffn",
    )(tile_expert, n_active, xs, wg, wu, wd)


def _combine_kernel(starts_ref, meta_ref, gate_ref, h1_ref, gfin_ref, y_hbm, out_ref, ya_ref, yb_ref, sem):
    rows = h1_ref.shape[0]

    def slots(r):
        return (starts_ref[meta_ref[0, r]] + meta_ref[2, r],
                starts_ref[meta_ref[1, r]] + meta_ref[3, r])

    def issue(r, c):
        p1, p2 = slots(r)
        _row_copy(y_hbm, p1, ya_ref, r, sem).start(priority=0)
        _row_copy(y_hbm, p2, yb_ref, r, sem).start(priority=1)
        return c

    def drain(r, c):
        p1, p2 = slots(r)
        _row_copy(y_hbm, p1, ya_ref, r, sem).wait()
        _row_copy(y_hbm, p2, yb_ref, r, sem).wait()
        return c

    lax.fori_loop(0, rows, issue, 0, unroll=DMA_UNROLL)
    lax.fori_loop(0, rows, drain, 0, unroll=DMA_UNROLL)
    gates = gate_ref[...]
    h2 = h1_ref[...] + (gates[:, 0:1] * _from_row_tiles(ya_ref) + gates[:, 1:2] * _from_row_tiles(yb_ref))
    out_ref[...] = _rmsnorm(h2, gfin_ref[...])


def _combine(starts, meta, gates, h1, gfin, y):
    t, d = h1.shape
    n_tiles = meta.shape[0]
    return pl.pallas_call(
        _combine_kernel,
        out_shape=jax.ShapeDtypeStruct((t, d), F32),
        grid_spec=pltpu.PrefetchScalarGridSpec(
            num_scalar_prefetch=1,
            grid=(n_tiles,),
            in_specs=[pl.BlockSpec((None, 8, ROW_TILE), lambda i, *_: (i, 0, 0), memory_space=pltpu.SMEM),
                      pl.BlockSpec((ROW_TILE, LANES), lambda i, *_: (i, 0)),
                      pl.BlockSpec((ROW_TILE, d), lambda i, *_: (i, 0)),
                      pl.BlockSpec((1, d), lambda i, *_: (0, 0)),
                      pl.BlockSpec(memory_space=pl.ANY)],
            out_specs=pl.BlockSpec((ROW_TILE, d), lambda i, *_: (i, 0)),
            scratch_shapes=[pltpu.VMEM((ROW_TILE * SUBLANES, LANES), F32),
                            pltpu.VMEM((ROW_TILE * SUBLANES, LANES), F32),
                            pltpu.SemaphoreType.DMA(())]),
        compiler_params=_cparams(("arbitrary",)),
        name="moe_combine",
    )(starts, meta, gates, h1, gfin.reshape(1, d), y)


def kernel(x, meta_tokens, norm_gains, sb_w_qkv, sb_w_o, diff_w_qkv, diff_w_o, diff_subln,
           diff_lambda_q1, diff_lambda_k1, diff_lambda_q2, diff_lambda_k2,
           ffn_w_gate, ffn_w_up, ffn_w_down, moe_router, moe_w_gate, moe_w_up, moe_w_down,
           final_norm):
    batch, seq_in, d = x.shape
    n_meta = meta_tokens.shape[0]
    seq = n_meta + seq_in
    t = batch * seq
    n_exp = moe_router.shape[-1]
    assert d % LANES == 0 and t % ROW_TILE == 0 and seq % 16 == 0 and seq >= TQ and seq % TQ < TK
    assert diff_subln.shape[-1] == LANES and n_exp <= 8

    meta = jnp.broadcast_to(meta_tokens.astype(x.dtype)[None], (batch, n_meta, d))
    h = jnp.concatenate([meta, x], axis=1).reshape(t, d)

    qkv = _norm_matmul(h, norm_gains[0, 0], sb_w_qkv[0].astype(BF16))
    o = _sb_attention(qkv, batch, seq, d)
    h = _wo_ffn(o, h, sb_w_o[0].astype(BF16), norm_gains[0, 1],
                ffn_w_gate[0].astype(BF16), ffn_w_up[0].astype(BF16), ffn_w_down[0].astype(BF16))

    lambda_init = 0.8 - 0.6 * math.exp(-0.3 * 1)
    qkv = _norm_matmul(h, norm_gains[1, 0], diff_w_qkv[0].astype(BF16))
    lam_params = jnp.stack([diff_lambda_q1[0], diff_lambda_k1[0], diff_lambda_q2[0], diff_lambda_k2[0]])
    o = _diff_attention(qkv, lam_params, diff_subln[0], batch, seq, d, lambda_init)
    h1, hn, gates, route, counts = _wo_router(o, h, diff_w_o[0].astype(BF16), norm_gains[1, 1],
                                              moe_router[0], n_exp)

    counts = counts[0, :n_exp].astype(jnp.int32)
    tiles_per = (counts + EXPERT_TILE - 1) // EXPERT_TILE
    tile_end = jnp.cumsum(tiles_per)
    starts = jnp.concatenate([jnp.zeros((1,), jnp.int32), tile_end * EXPERT_TILE]).astype(jnp.int32)
    n_tiles = (TOP_K_SLOTS * t + EXPERT_TILE - 1) // EXPERT_TILE + n_exp
    n_active = tile_end[-1:].astype(jnp.int32)
    tile_ids = jnp.minimum(jnp.arange(n_tiles, dtype=jnp.int32), n_active[0] - 1)
    tile_expert = jnp.sum((tile_ids[:, None] >= tile_end[None, :]).astype(jnp.int32), axis=1)

    xs = _dispatch(starts, counts, route, hn, n_tiles * EXPERT_TILE, n_exp)
    y = _expert_ffn(tile_expert, n_active, xs, moe_w_gate[0].astype(BF16), moe_w_up[0].astype(BF16),
                    moe_w_down[0].astype(BF16))
    out = _combine(starts, route, gates, h1, final_norm, y)
    return out.reshape(batch, seq, d)[:, n_meta:]
```

```python
import functools
import math

import jax
import jax.numpy as jnp
from jax import lax
from jax.experimental import pallas as pl
from jax.experimental.pallas import tpu as pltpu

F32 = jnp.float32
BF16 = jnp.bfloat16

NORM_EPS = 1e-6
ROPE_THETA = 10000.0
HEAD_DIM = 64
LANES = 128
SUBLANES = 8
ZERO_ROWS = 8
QB = 128
TQ = 512
SB_TQ = 512
TK = 256
DIFF_TK = 512
SB_GROUP = 4
ATTN_LANES = 512
ROW_TILE = 512
EXPERT_TILE = 1024
EXPERT_FF_CHUNK = 1792
FF_CHUNK = 256
DMA_UNROLL = 8
TOP_K_SLOTS = 2
NEG_BIG = -1e30
LOG2E = 1.4426950408889634
VMEM_LIMIT = 56 * 1024 * 1024


def _cparams(sem):
    return pltpu.CompilerParams(dimension_semantics=sem, vmem_limit_bytes=VMEM_LIMIT)


def _rmsnorm(x, g):
    ms = jnp.mean(x * x, axis=-1, keepdims=True)
    return x * lax.rsqrt(ms + NORM_EPS) * g


def _dot(a, b):
    return jnp.dot(a, b, preferred_element_type=F32)


def _dot_t(a, b):
    return lax.dot_general(a, b, (((1,), (1,)), ((), ())), preferred_element_type=F32)


def _split_bf16(x):
    hi = x.astype(BF16)
    lo = (x - hi.astype(F32)).astype(BF16)
    return hi, lo


def _resident(shape):
    nd = len(shape)
    return pl.BlockSpec(shape, lambda *_: (0,) * nd, pipeline_mode=pl.Buffered(1))


def _norm_matmul_kernel(h_ref, g_ref, w_ref, o_ref, *, n_chunk):
    hn = _rmsnorm(h_ref[...], g_ref[...]).astype(BF16)
    for c in range(0, w_ref.shape[1], n_chunk):
        o_ref[:, c:c + n_chunk] = _dot(hn, w_ref[:, c:c + n_chunk]).astype(o_ref.dtype)


def _norm_matmul(h, g, w):
    t, d = h.shape
    n = w.shape[1]
    return pl.pallas_call(
        functools.partial(_norm_matmul_kernel, n_chunk=min(n, 1024)),
        out_shape=jax.ShapeDtypeStruct((t, n), BF16),
        grid=(t // ROW_TILE,),
        in_specs=[pl.BlockSpec((ROW_TILE, d), lambda i: (i, 0)),
                  _resident((1, d)),
                  _resident((d, n))],
        out_specs=pl.BlockSpec((ROW_TILE, n), lambda i: (i, 0)),
        compiler_params=_cparams(("parallel",)),
        name="norm_qkv",
    )(h, g.reshape(1, d), w)


def _stacked_idx(rows, width):
    rr = lax.broadcasted_iota(jnp.int32, (2 * rows, width), 0)
    cc = lax.broadcasted_iota(jnp.int32, (2 * rows, width), 1)
    return jnp.where(rr >= rows, rr - rows, rr), cc


def _sb_attn_kernel(q_ref, k_ref, v_ref, o_ref, acc_ref, *, seq, scale):
    n_pairs = q_ref.shape[1] // LANES
    nq = seq // SB_TQ
    tail = seq - nq * SB_TQ
    low_lanes = lax.broadcasted_iota(jnp.int32, (1, LANES), 1) < HEAD_DIM
    j_loc, s_loc = _stacked_idx(QB, QB)
    from_s = (j_loc >= s_loc).astype(BF16)

    def cols(pr):
        return slice(pr * LANES, (pr + 1) * LANES)

    def stack_heads(x):
        zero = jnp.zeros_like(x)
        return jnp.concatenate([jnp.where(low_lanes, x, zero), jnp.where(low_lanes, zero, x)], axis=0)

    def chunk(qzs, k0, cs, rows, mask):
        new_cs = []
        for g0 in range(0, n_pairs, SB_GROUP):
            group = range(g0, min(g0 + SB_GROUP, n_pairs))
            neg_zs, log_keeps = [], []
            for pr in group:
                nz = _dot_t(qzs[pr], k_ref[pl.ds(k0, TK), cols(pr)])
                soft = jnp.log(1.0 + jnp.exp2(jnp.abs(nz) * (-LOG2E)))
                log_keep = jnp.minimum(nz, 0.0) - soft
                if mask is not None:
                    log_keep = jnp.where(mask, log_keep, 0.0)
                neg_zs.append(nz)
                log_keeps.append(log_keep)
            halves = []
            for h0 in range(0, TK, QB):
                parts = []
                for lk in log_keeps:
                    hi, lo = _split_bf16(lk[:, h0:h0 + QB])
                    parts.append(jnp.concatenate([hi, lo], axis=1))
                halves.append(_dot(jnp.concatenate(parts, axis=0), from_s))
            for n, pr in enumerate(group):
                run = None if cs is None else cs[pr]
                ts = []
                for hh in reversed(range(TK // QB)):
                    keep_from = halves[hh][n * 2 * rows:(n + 1) * 2 * rows]
                    t = keep_from - neg_zs[n][:, hh * QB:(hh + 1) * QB]
                    if run is not None:
                        t = t + run
                    ts.append(t)
                    half_sum = keep_from[:, 0:1]
                    run = half_sum if run is None else run + half_sum
                p = jnp.exp2(jnp.concatenate(ts[::-1], axis=1) * LOG2E)
                if mask is not None:
                    p = jnp.where(mask, p, 0.0)
                pb = p.astype(BF16)
                contrib = _dot(jnp.concatenate([pb[:rows], pb[rows:]], axis=1),
                               stack_heads(v_ref[pl.ds(k0, TK), cols(pr)]))
                if cs is None:
                    acc_ref[0:rows, cols(pr)] = contrib
                else:
                    acc_ref[0:rows, cols(pr)] += contrib
                new_cs.append(run)
        return tuple(new_cs)

    def q_block(q0, rows, masked, n_below):
        qzs = []
        for pr in range(n_pairs):
            qs = (q_ref[pl.ds(q0, rows), cols(pr)].astype(F32) * (-scale)).astype(BF16)
            qzs.append(stack_heads(qs))
        cs = None
        for k0, mask in masked:
            cs = chunk(qzs, k0, cs, rows, mask)

        def below(idx, cs):
            return chunk(qzs, pl.multiple_of((n_below - 1 - idx) * TK, TK), cs, rows, None)

        lax.fori_loop(0, n_below, below, cs)
        o_ref[pl.ds(q0, rows), :] = acc_ref[0:rows, :].astype(o_ref.dtype)

    def q_loop(i, carry):
        q0 = pl.multiple_of(i * SB_TQ, SB_TQ)
        t_loc, s_loc = _stacked_idx(SB_TQ, TK)
        masked = []
        for m in reversed(range(SB_TQ // TK)):
            masked.append((q0 + m * TK, s_loc + m * TK < t_loc))
        q_block(q0, SB_TQ, masked, i * (SB_TQ // TK))
        return carry

    lax.fori_loop(0, nq, q_loop, 0)

    if tail:
        q0 = nq * SB_TQ
        k0 = seq - TK
        t_loc, s_loc = _stacked_idx(tail, TK)
        q_block(q0, tail, [(k0, (s_loc + k0 >= q0) & (s_loc + k0 < t_loc + q0))], q0 // TK)


def _attn_specs(seq, d_model):
    nblk = d_model // ATTN_LANES
    spec = lambda off: pl.BlockSpec((None, seq, ATTN_LANES), lambda b, c: (b, 0, off + c))
    return nblk, spec(0), spec(nblk), spec(2 * nblk)


def _sb_attention(qkv, batch, seq, d_model):
    qkv3 = qkv.reshape(batch, seq, 3 * d_model)
    nblk, q_spec, k_spec, v_spec = _attn_specs(seq, d_model)
    out = pl.pallas_call(
        functools.partial(_sb_attn_kernel, seq=seq, scale=HEAD_DIM ** -0.5),
        out_shape=jax.ShapeDtypeStruct((batch, seq, d_model), BF16),
        grid=(batch, nblk),
        in_specs=[q_spec, k_spec, v_spec],
        out_specs=q_spec,
        scratch_shapes=[pltpu.VMEM((SB_TQ, ATTN_LANES), F32)],
        compiler_params=_cparams(("parallel", "parallel")),
        name="sb_attention",
    )(qkv3, qkv3, qkv3)
    return out.reshape(batch * seq, d_model)


def _diff_attn_kernel(q_ref, k_ref, v_ref, cos_ref, sin_ref, lam_ref, subg_ref, o_ref, krot_ref, acc_ref,
                      *, seq, scale, lambda_init):
    n_heads = q_ref.shape[1] // LANES
    nq = seq // TQ
    tail = seq - nq * TQ
    lane = lax.broadcasted_iota(jnp.int32, (1, LANES), 1)
    low_lanes = lane < HEAD_DIM
    first_half = (lane & (HEAD_DIM // 2)) == 0

    lam_p = lam_ref[...]
    lam = (jnp.exp(jnp.sum(lam_p[0:1] * lam_p[1:2], axis=-1, keepdims=True))
           - jnp.exp(jnp.sum(lam_p[2:3] * lam_p[3:4], axis=-1, keepdims=True))
           + lambda_init)

    def cols(h):
        return slice(h * LANES, (h + 1) * LANES)

    def rope(x, r0, rows):
        partner = jnp.where(first_half, pltpu.roll(x, LANES - HEAD_DIM // 2, 1),
                            pltpu.roll(x, HEAD_DIM // 2, 1))
        return x * cos_ref[pl.ds(r0, rows), :] + partner * sin_ref[pl.ds(r0, rows), :]

    def rot_keys(r0, rows):
        for h in range(n_heads):
            kh = k_ref[pl.ds(r0, rows), cols(h)].astype(F32)
            krot_ref[pl.ds(r0, rows), cols(h)] = rope(kh, r0, rows).astype(BF16)

    def rot_loop(i, carry):
        rot_keys(pl.multiple_of(i * QB, QB), QB)
        return carry

    lax.fori_loop(0, seq // QB, rot_loop, 0)
    if seq % QB:
        rot_keys(seq - seq % QB, seq % QB)

    def stack_maps(x):
        zero = jnp.zeros_like(x)
        return jnp.concatenate([jnp.where(low_lanes, x, zero), jnp.where(low_lanes, zero, x)], axis=0)

    def chunk(qzs, k0, st, rows, mask):
        out = []
        for h in range(n_heads):
            s = _dot_t(qzs[h], krot_ref[pl.ds(k0, DIFF_TK), cols(h)])
            if mask is not None:
                s = jnp.where(mask, s, NEG_BIG)
            mx = jnp.max(s, axis=1, keepdims=True)
            vb = v_ref[pl.ds(k0, DIFF_TK), cols(h)]
            if st is None:
                p = jnp.exp(s - mx)
                acc_ref[h, 0:2 * rows, :] = _dot(p.astype(BF16), vb)
                out.extend([mx, jnp.sum(p, axis=1, keepdims=True)])
            else:
                m, l = st[2 * h], st[2 * h + 1]
                m_new = jnp.maximum(m, mx)
                alpha = jnp.exp(m - m_new)
                p = jnp.exp(s - m_new)
                acc_ref[h, 0:2 * rows, :] = alpha * acc_ref[h, 0:2 * rows, :] + _dot(p.astype(BF16), vb)
                out.extend([m_new, alpha * l + jnp.sum(p, axis=1, keepdims=True)])
        return tuple(out)

    def q_block(q0, rows, masked, n_below):
        qzs = []
        for h in range(n_heads):
            qh = rope(q_ref[pl.ds(q0, rows), cols(h)].astype(F32), q0, rows) * scale
            qzs.append(stack_maps(qh.astype(BF16)))
        st = None
        for k0, mask in masked:
            st = chunk(qzs, k0, st, rows, mask)

        def below(idx, st):
            return chunk(qzs, pl.multiple_of(idx * DIFF_TK, DIFF_TK), st, rows, None)

        st = lax.fori_loop(0, n_below, below, st)
        for h in range(n_heads):
            a = acc_ref[h, 0:2 * rows, :] / st[2 * h + 1]
            o = a[:rows] - lam * a[rows:]
            o = _rmsnorm(o, subg_ref[...]) * (1.0 - lambda_init)
            o_ref[pl.ds(q0, rows), cols(h)] = o.astype(o_ref.dtype)

    def q_loop(i, carry):
        q0 = pl.multiple_of(i * TQ, TQ)
        t_loc, s_loc = _stacked_idx(TQ, DIFF_TK)
        masked = [(q0 + m * DIFF_TK, s_loc + m * DIFF_TK <= t_loc) for m in range(TQ // DIFF_TK)]
        q_block(q0, TQ, masked, i * (TQ // DIFF_TK))
        return carry

    lax.fori_loop(0, nq, q_loop, 0)

    if tail:
        q0 = nq * TQ
        k0 = seq - DIFF_TK
        t_loc, s_loc = _stacked_idx(tail, DIFF_TK)
        q_block(q0, tail, [(k0, (s_loc + k0 >= q0) & (s_loc + k0 <= t_loc + q0))], q0 // DIFF_TK)


def _rope_tables(seq):
    inv = ROPE_THETA ** (-jnp.arange(0, HEAD_DIM, 2, dtype=F32) / HEAD_DIM)
    ang = jnp.arange(seq, dtype=F32)[:, None] * inv[None, :]
    cos, sin = jnp.cos(ang), jnp.sin(ang)
    reps = LANES // HEAD_DIM
    cos_t = jnp.tile(cos, (1, 2 * reps))
    sin_t = jnp.tile(jnp.concatenate([-sin, sin], axis=1), (1, reps))
    assert cos_t.shape == (seq, LANES) and sin_t.shape == (seq, LANES)
    return cos_t, sin_t


def _diff_attention(qkv, lam_params, subln_g, batch, seq, d_model, lambda_init):
    qkv3 = qkv.reshape(batch, seq, 3 * d_model)
    cos_t, sin_t = _rope_tables(seq)
    nblk, q_spec, k_spec, v_spec = _attn_specs(seq, d_model)
    out = pl.pallas_call(
        functools.partial(_diff_attn_kernel, seq=seq, scale=HEAD_DIM ** -0.5, lambda_init=lambda_init),
        out_shape=jax.ShapeDtypeStruct((batch, seq, d_model), BF16),
        grid=(batch, nblk),
        in_specs=[q_spec, k_spec, v_spec,
                  _resident((seq, LANES)), _resident((seq, LANES)),
                  _resident(lam_params.shape), _resident((1, LANES))],
        out_specs=q_spec,
        scratch_shapes=[pltpu.VMEM((seq, ATTN_LANES), BF16),
                        pltpu.VMEM((ATTN_LANES // LANES, 2 * TQ, LANES), F32)],
        compiler_params=_cparams(("parallel", "parallel")),
        name="diff_attention",
    )(qkv3, qkv3, qkv3, cos_t, sin_t, lam_params, subln_g.reshape(1, LANES))
    return out.reshape(batch * seq, d_model)


def _silu_mul(g, u):
    return g * (1.0 / (1.0 + jnp.exp(-g))) * u


def _wo_ffn_kernel(o_ref, h_ref, wo_ref, g_ref, wg_ref, wu_ref, wd_ref, out_ref, a_ref, *, f_chunk):
    h1 = h_ref[...] + _dot(o_ref[...], wo_ref[...])
    hn = _rmsnorm(h1, g_ref[...]).astype(BF16)
    for c in range(0, wg_ref.shape[1], f_chunk):
        gate = _dot(hn, wg_ref[:, c:c + f_chunk])
        up = _dot(hn, wu_ref[:, c:c + f_chunk])
        a_ref[:, c:c + f_chunk] = _silu_mul(gate, up).astype(BF16)
    out_ref[...] = h1 + _dot(a_ref[...], wd_ref[...])


def _wo_ffn(o, h, wo, g, wg, wu, wd):
    t, d = h.shape
    f = wg.shape[1]
    f_chunk = FF_CHUNK if f % FF_CHUNK == 0 else f
    row = lambda: pl.BlockSpec((ROW_TILE, d), lambda i: (i, 0))
    return pl.pallas_call(
        functools.partial(_wo_ffn_kernel, f_chunk=f_chunk),
        out_shape=jax.ShapeDtypeStruct((t, d), F32),
        grid=(t // ROW_TILE,),
        in_specs=[row(), row(), _resident((d, d)), _resident((1, d)),
                  _resident((d, f)), _resident((d, f)), _resident((f, d))],
        out_specs=row(),
        scratch_shapes=[pltpu.VMEM((ROW_TILE, f), BF16)],
        compiler_params=_cparams(("parallel",)),
        name="wo_ffn",
    )(o, h, wo, g.reshape(1, d), wg, wu, wd)


def _wo_router_kernel(o_ref, h_ref, wo_ref, g_ref, wr_hi_ref, wr_lo_ref,
                      h1_ref, hn_ref, gate_ref, meta_ref, cnt_ref, run_ref, *, n_exp):
    rows = h_ref.shape[0]

    @pl.when(pl.program_id(0) == 0)
    def _():
        run_ref[...] = jnp.zeros_like(run_ref)

    h1 = h_ref[...] + _dot(o_ref[...], wo_ref[...])
    h1_ref[...] = h1
    hn = _rmsnorm(h1, g_ref[...])
    hn_ref[...] = hn

    x_hi, x_lo = _split_bf16(hn)
    logits = _dot(x_hi, wr_hi_ref[...]) + (_dot(x_lo, wr_hi_ref[...]) + _dot(x_hi, wr_lo_ref[...]))
    lane = lax.broadcasted_iota(jnp.int32, (rows, LANES), 1)
    logits = jnp.where(lane < n_exp, logits, NEG_BIG)

    def top1(lg):
        m = jnp.max(lg, axis=1, keepdims=True)
        idx = jnp.min(jnp.where(lg == m, lane, LANES), axis=1, keepdims=True)
        return m, idx

    v1, i1 = top1(logits)
    v2, i2 = top1(jnp.where(lane == i1, NEG_BIG, logits))
    e = jnp.exp(v2 - v1)
    g1 = 1.0 / (1.0 + e)
    g2 = e / (1.0 + e)

    oh1 = (lane == i1).astype(F32)
    oh2 = (lane == i2).astype(F32)
    both = oh1 + oh2
    rr = lax.broadcasted_iota(jnp.int32, (rows, rows), 0)
    cc = lax.broadcasted_iota(jnp.int32, (rows, rows), 1)
    earlier = (cc < rr).astype(BF16)
    base = run_ref[...] + _dot(earlier, both.astype(BF16))
    rank1 = jnp.sum(oh1 * base, axis=1, keepdims=True)
    rank2 = jnp.sum(oh2 * base, axis=1, keepdims=True)
    run_ref[...] = run_ref[...] + jnp.sum(both, axis=0, keepdims=True)
    cnt_ref[...] = run_ref[...]

    gate_ref[...] = jnp.where(lane == 0, g1, jnp.where(lane == 1, g2, 0.0))
    packed = jnp.where(lane == 0, i1.astype(F32),
                       jnp.where(lane == 1, i2.astype(F32),
                                 jnp.where(lane == 2, rank1, jnp.where(lane == 3, rank2, 0.0))))
    meta_ref[...] = jnp.transpose(packed)[0:8, :].astype(jnp.int32)


def _wo_router(o, h, wo, g, w_router, n_exp):
    t, d = h.shape
    n_tiles = t // ROW_TILE
    wr = jnp.zeros((d, LANES), F32).at[:, :n_exp].set(w_router)
    wr_hi = wr.astype(BF16)
    wr_lo = (wr - wr_hi.astype(F32)).astype(BF16)
    row = lambda: pl.BlockSpec((ROW_TILE, d), lambda i: (i, 0))
    return pl.pallas_call(
        functools.partial(_wo_router_kernel, n_exp=n_exp),
        out_shape=(jax.ShapeDtypeStruct((t, d), F32),
                   jax.ShapeDtypeStruct((t, d), F32),
                   jax.ShapeDtypeStruct((t, LANES), F32),
                   jax.ShapeDtypeStruct((n_tiles, 8, ROW_TILE), jnp.int32),
                   jax.ShapeDtypeStruct((1, LANES), F32)),
        grid=(n_tiles,),
        in_specs=[row(), row(), _resident((d, d)), _resident((1, d)),
                  _resident((d, LANES)), _resident((d, LANES))],
        out_specs=(row(), row(),
                   pl.BlockSpec((ROW_TILE, LANES), lambda i: (i, 0)),
                   pl.BlockSpec((None, 8, ROW_TILE), lambda i: (i, 0, 0)),
                   pl.BlockSpec((1, LANES), lambda i: (0, 0))),
        scratch_shapes=[pltpu.VMEM((1, LANES), F32)],
        compiler_params=_cparams(("arbitrary",)),
        name="wo_router",
    )(o, h, wo, g.reshape(1, d), wr_hi, wr_lo)


def _row_copy(src, s, dst, d, sem):
    return pltpu.make_async_copy(src.at[pl.ds(pl.multiple_of(s * SUBLANES, SUBLANES), SUBLANES)],
                                 dst.at[pl.ds(pl.multiple_of(d * SUBLANES, SUBLANES), SUBLANES)], sem)


def _to_row_tiles(dst_ref, x):
    n = x.shape[0]
    for c in range(SUBLANES):
        dst_ref[pl.ds(c, n, stride=SUBLANES), :] = x[:, c * LANES:(c + 1) * LANES]


def _from_row_tiles(src_ref):
    n = src_ref.shape[0] // SUBLANES
    return jnp.concatenate([src_ref[pl.ds(c, n, stride=SUBLANES), :] for c in range(SUBLANES)], axis=1)


def _dispatch_kernel(starts_ref, counts_ref, meta_ref, hn_ref, xs_hbm, stage_ref, zero_ref, sem, *, n_exp):
    i = pl.program_id(0)
    rows = meta_ref.shape[1]
    zrows = zero_ref.shape[0] // SUBLANES
    n_sorted = xs_hbm.shape[0] // SUBLANES

    @pl.when(i == 0)
    def _():
        zero_ref[...] = jnp.zeros_like(zero_ref)
        used = starts_ref[n_exp]
        n_blocks = (n_sorted - used) // zrows

        def tail_copy(r):
            first_row = pl.multiple_of((used + r * zrows) * SUBLANES, zrows * SUBLANES)
            return pltpu.make_async_copy(zero_ref, xs_hbm.at[pl.ds(first_row, zrows * SUBLANES)], sem)

        def tail_fill(r, c):
            tail_copy(r).start()
            return c

        def tail_drain(r, c):
            tail_copy(r).wait()
            return c

        lax.fori_loop(0, n_blocks, tail_fill, 0)
        lax.fori_loop(0, n_blocks, tail_drain, 0)

        for e in range(n_exp):
            first = starts_ref[e] + counts_ref[e]
            n_pad = starts_ref[e + 1] - first

            def fill(r, c):
                _row_copy(zero_ref, 0, xs_hbm, first + r, sem).start()
                return c

            def drain(r, c):
                _row_copy(zero_ref, 0, xs_hbm, first + r, sem).wait()
                return c

            lax.fori_loop(0, n_pad, fill, 0)
            lax.fori_loop(0, n_pad, drain, 0)

    def slots(r):
        return (starts_ref[meta_ref[0, r]] + meta_ref[2, r],
                starts_ref[meta_ref[1, r]] + meta_ref[3, r])

    _to_row_tiles(stage_ref, hn_ref[...])

    def issue(r, c):
        p1, p2 = slots(r)
        _row_copy(stage_ref, r, xs_hbm, p1, sem).start(priority=0)
        _row_copy(stage_ref, r, xs_hbm, p2, sem).start(priority=1)
        return c

    def drain(r, c):
        p1, p2 = slots(r)
        _row_copy(stage_ref, r, xs_hbm, p1, sem).wait()
        _row_copy(stage_ref, r, xs_hbm, p2, sem).wait()
        return c

    lax.fori_loop(0, rows, issue, 0, unroll=DMA_UNROLL)
    lax.fori_loop(0, rows, drain, 0, unroll=DMA_UNROLL)


def _dispatch(starts, counts, meta, hn, n_sorted, n_exp):
    t, d = hn.shape
    assert d == SUBLANES * LANES
    n_tiles = meta.shape[0]
    return pl.pallas_call(
        functools.partial(_dispatch_kernel, n_exp=n_exp),
        out_shape=jax.ShapeDtypeStruct((n_sorted * SUBLANES, LANES), F32),
        grid_spec=pltpu.PrefetchScalarGridSpec(
            num_scalar_prefetch=2,
            grid=(n_tiles,),
            in_specs=[pl.BlockSpec((None, 8, ROW_TILE), lambda i, *_: (i, 0, 0), memory_space=pltpu.SMEM),
                      pl.BlockSpec((ROW_TILE, d), lambda i, *_: (i, 0))],
            out_specs=pl.BlockSpec(memory_space=pl.ANY),
            scratch_shapes=[pltpu.VMEM((ROW_TILE * SUBLANES, LANES), F32),
                            pltpu.VMEM((ZERO_ROWS * SUBLANES, LANES), F32),
                            pltpu.SemaphoreType.DMA(())]),
        compiler_params=_cparams(("arbitrary",)),
        name="moe_dispatch",
    )(starts, counts, meta, hn)


def _expert_ffn_kernel(te_ref, nact_ref, xs_ref, wg_ref, wu_ref, wd_ref, y_ref, xb_ref, a_ref, acc_ref):
    i = pl.program_id(0)
    j = pl.program_id(1)

    @pl.when(i < nact_ref[0])
    def _():
        @pl.when(j == 0)
        def _():
            xb_ref[...] = _from_row_tiles(xs_ref).astype(BF16)
            acc_ref[...] = jnp.zeros_like(acc_ref)

        xb = xb_ref[...]
        for c in range(0, a_ref.shape[1], FF_CHUNK):
            gate = _dot(xb, wg_ref[:, c:c + FF_CHUNK])
            up = _dot(xb, wu_ref[:, c:c + FF_CHUNK])
            a_ref[:, c:c + FF_CHUNK] = _silu_mul(gate, up).astype(BF16)
        acc_ref[...] += _dot(a_ref[...], wd_ref[...])

        @pl.when(j == pl.num_programs(1) - 1)
        def _():
            _to_row_tiles(y_ref, acc_ref[...])

    @pl.when((i >= nact_ref[0]) & (j == pl.num_programs(1) - 1))
    def _():
        y_ref[...] = jnp.zeros_like(y_ref)


def _expert_ffn(tile_expert, n_active, xs, wg, wu, wd):
    d = SUBLANES * LANES
    n_tiles = xs.shape[0] // (EXPERT_TILE * SUBLANES)
    f = wg.shape[2]
    fc = EXPERT_FF_CHUNK if f % EXPERT_FF_CHUNK == 0 else f
    nf = f // fc

    def tile(i, na):
        return jnp.minimum(i, na[0] - 1)

    def fchunk(i, j, na):
        return jnp.where(i < na[0], j, nf - 1)

    return pl.pallas_call(
        _expert_ffn_kernel,
        out_shape=jax.ShapeDtypeStruct(xs.shape, F32),
        grid_spec=pltpu.PrefetchScalarGridSpec(
            num_scalar_prefetch=2,
            grid=(n_tiles, nf),
            in_specs=[pl.BlockSpec((EXPERT_TILE * SUBLANES, LANES), lambda i, j, te, na: (tile(i, na), 0)),
                      pl.BlockSpec((None, d, fc), lambda i, j, te, na: (te[i], 0, fchunk(i, j, na))),
                      pl.BlockSpec((None, d, fc), lambda i, j, te, na: (te[i], 0, fchunk(i, j, na))),
                      pl.BlockSpec((None, fc, d), lambda i, j, te, na: (te[i], fchunk(i, j, na), 0))],
            out_specs=pl.BlockSpec((EXPERT_TILE * SUBLANES, LANES), lambda i, j, te, na: (i, 0)),
            scratch_shapes=[pltpu.VMEM((EXPERT_TILE, d), BF16), pltpu.VMEM((EXPERT_TILE, fc), BF16),
                            pltpu.VMEM((EXPERT_TILE, d), F32)]),
        compiler_params=_cparams(("arbitrary", "arbitrary")),
        name="expert_ffn",
    )(tile_expert, n_active, xs, wg, wu, wd)


def _combine_kernel(starts_ref, meta_ref, gate_ref, h1_ref, gfin_ref, y_hbm, out_ref, ya_ref, yb_ref, sem):
    rows = h1_ref.shape[0]

    def slots(r):
        return (starts_ref[meta_ref[0, r]] + meta_ref[2, r],
                starts_ref[meta_ref[1, r]] + meta_ref[3, r])

    def issue(r, c):
        p1, p2 = slots(r)
        _row_copy(y_hbm, p1, ya_ref, r, sem).start(priority=0)
        _row_copy(y_hbm, p2, yb_ref, r, sem).start(priority=1)
        return c

    def drain(r, c):
        p1, p2 = slots(r)
        _row_copy(y_hbm, p1, ya_ref, r, sem).wait()
        _row_copy(y_hbm, p2, yb_ref, r, sem).wait()
        return c

    lax.fori_loop(0, rows, issue, 0, unroll=DMA_UNROLL)
    lax.fori_loop(0, rows, drain, 0, unroll=DMA_UNROLL)
    gates = gate_ref[...]
    h2 = h1_ref[...] + (gates[:, 0:1] * _from_row_tiles(ya_ref) + gates[:, 1:2] * _from_row_tiles(yb_ref))
    out_ref[...] = _rmsnorm(h2, gfin_ref[...])


def _combine(starts, meta, gates, h1, gfin, y):
    t, d = h1.shape
    n_tiles = meta.shape[0]
    return pl.pallas_call(
        _combine_kernel,
        out_shape=jax.ShapeDtypeStruct((t, d), F32),
        grid_spec=pltpu.PrefetchScalarGridSpec(
            num_scalar_prefetch=1,
            grid=(n_tiles,),
            in_specs=[pl.BlockSpec((None, 8, ROW_TILE), lambda i, *_: (i, 0, 0), memory_space=pltpu.SMEM),
                      pl.BlockSpec((ROW_TILE, LANES), lambda i, *_: (i, 0)),
                      pl.BlockSpec((ROW_TILE, d), lambda i, *_: (i, 0)),
                      pl.BlockSpec((1, d), lambda i, *_: (0, 0)),
                      pl.BlockSpec(memory_space=pl.ANY)],
            out_specs=pl.BlockSpec((ROW_TILE, d), lambda i, *_: (i, 0)),
            scratch_shapes=[pltpu.VMEM((ROW_TILE * SUBLANES, LANES), F32),
                            pltpu.VMEM((ROW_TILE * SUBLANES, LANES), F32),
                            pltpu.SemaphoreType.DMA(())]),
        compiler_params=_cparams(("arbitrary",)),
        name="moe_combine",
    )(starts, meta, gates, h1, gfin.reshape(1, d), y)


def kernel(x, meta_tokens, norm_gains, sb_w_qkv, sb_w_o, diff_w_qkv, diff_w_o, diff_subln,
           diff_lambda_q1, diff_lambda_k1, diff_lambda_q2, diff_lambda_k2,
           ffn_w_gate, ffn_w_up, ffn_w_down, moe_router, moe_w_gate, moe_w_up, moe_w_down,
           final_norm):
    batch, seq_in, d = x.shape
    n_meta = meta_tokens.shape[0]
    seq = n_meta + seq_in
    t = batch * seq
    n_exp = moe_router.shape[-1]
    assert d % LANES == 0 and t % ROW_TILE == 0 and seq % 16 == 0 and seq >= TQ and seq % TQ < TK
    assert diff_subln.shape[-1] == LANES and n_exp <= 8

    meta = jnp.broadcast_to(meta_tokens.astype(x.dtype)[None], (batch, n_meta, d))
    h = jnp.concatenate([meta, x], axis=1).reshape(t, d)

    qkv = _norm_matmul(h, norm_gains[0, 0], sb_w_qkv[0].astype(BF16))
    o = _sb_attention(qkv, batch, seq, d)
    h = _wo_ffn(o, h, sb_w_o[0].astype(BF16), norm_gains[0, 1],
                ffn_w_gate[0].astype(BF16), ffn_w_up[0].astype(BF16), ffn_w_down[0].astype(BF16))

    lambda_init = 0.8 - 0.6 * math.exp(-0.3 * 1)
    qkv = _norm_matmul(h, norm_gains[1, 0], diff_w_qkv[0].astype(BF16))
    lam_params = jnp.stack([diff_lambda_q1[0], diff_lambda_k1[0], diff_lambda_q2[0], diff_lambda_k2[0]])
    o = _diff_attention(qkv, lam_params, diff_subln[0], batch, seq, d, lambda_init)
    h1, hn, gates, route, counts = _wo_router(o, h, diff_w_o[0].astype(BF16), norm_gains[1, 1],
                                              moe_router[0], n_exp)

    counts = counts[0, :n_exp].astype(jnp.int32)
    tiles_per = (counts + EXPERT_TILE - 1) // EXPERT_TILE
    tile_end = jnp.cumsum(tiles_per)
    starts = jnp.concatenate([jnp.zeros((1,), jnp.int32), tile_end * EXPERT_TILE]).astype(jnp.int32)
    n_tiles = (TOP_K_SLOTS * t + EXPERT_TILE - 1) // EXPERT_TILE + n_exp
    n_active = tile_end[-1:].astype(jnp.int32)
    tile_ids = jnp.minimum(jnp.arange(n_tiles, dtype=jnp.int32), n_active[0] - 1)
    tile_expert = jnp.sum((tile_ids[:, None] >= tile_end[None, :]).astype(jnp.int32), axis=1)

    xs = _dispatch(starts, counts, route, hn, n_tiles * EXPERT_TILE, n_exp)
    y = _expert_ffn(tile_expert, n_active, xs, moe_w_gate[0].astype(BF16), moe_w_up[0].astype(BF16),
                    moe_w_down[0].astype(BF16))
    out = _combine(starts, route, gates, h1, final_norm, y)
    return out.reshape(batch, seq, d)[:, n_meta:]
```
